```python
import functools
import jax, jax.numpy as jnp
from jax import lax
import numpy as np

D_MODEL = 1024
BATCH = 8
SEQ = 8192
DEPTH = 1
DEC_BATCH = 128
DEC_SEQ = 4
PAST_LEN = 8192
PAGE_SIZE = 128

HEAD_DIM = 64
MIX_WIDTH = D_MODEL
SB_HEADS = MIX_WIDTH // (2 * HEAD_DIM)
SB_WIDTH = SB_HEADS * HEAD_DIM
SB_LOGIT_OFFSET = -8.0
CONV_CH = MIX_WIDTH // 4
CONV_K = 31
MEM_HEADS = 4
MEM_WIDTH = MEM_HEADS * HEAD_DIM
MEM_TOKENS = 256
Q_BLOCK = 128
OFF_K = SB_WIDTH
OFF_V = 2 * SB_WIDTH
OFF_GLU_A = 3 * SB_WIDTH
OFF_GLU_B = OFF_GLU_A + CONV_CH
OFF_MEM_Q = OFF_GLU_B + CONV_CH
IN_COLS = OFF_MEM_Q + MEM_WIDTH
PEER_HEADS = 8
N_KEYS = 128
N_EXPERTS = N_KEYS * N_KEYS
PEER_TOPK = 16
PEER_DKEY = 256
PEER_DHALF = PEER_DKEY // 2
PEER_CHUNK = 256
EPS = 1e-6

kernel_name = 'hybrid_sb_conformer_mem_peer_step'


def rms_norm(x, g):
    xf = x.astype(jnp.float32)
    y = xf * lax.rsqrt(jnp.mean(xf * xf, axis=-1, keepdims=True) + EPS)
    return (y * g.astype(jnp.float32)).astype(x.dtype)


def layer_norm(x, g, b):
    xf = x.astype(jnp.float32)
    mu = jnp.mean(xf, axis=-1, keepdims=True)
    xc = xf - mu
    y = xc * lax.rsqrt(jnp.mean(xc * xc, axis=-1, keepdims=True) + EPS)
    return (y * g.astype(jnp.float32) + b.astype(jnp.float32)).astype(x.dtype)


def stick_breaking(q, k, v, bias, q_pos, k_pos):
    z = jnp.einsum('bqhd,bkhd->bhqk', q.astype(jnp.float32), k.astype(jnp.float32)) * (HEAD_DIM ** -0.5)
    z = z + bias.astype(jnp.float32)[None, :, None, None]
    causal = k_pos[None, :] < q_pos[:, None]
    log_beta = jax.nn.log_sigmoid(z)
    log_1m = jnp.where(causal, jax.nn.log_sigmoid(-z), 0.0)
    rev = lax.cumsum(log_1m, axis=3, reverse=True)
    tail = jnp.concatenate([rev[..., 1:], jnp.zeros_like(rev[..., :1])], axis=-1)
    w = jnp.where(causal, jnp.exp(log_beta + tail), 0.0)
    return jnp.einsum('bhqk,bkhd->bqhd', w, v.astype(jnp.float32)).astype(q.dtype)


def sb_prompt(q, k, v, bias):
    b, s, h, d = q.shape
    n_blk = s // Q_BLOCK
    q_blocks = jnp.moveaxis(q.reshape(b, n_blk, Q_BLOCK, h, d), 1, 0)
    k_pos = jnp.arange(s)

    def block(args):
        i, q_i = args
        q_pos = i * Q_BLOCK + jnp.arange(Q_BLOCK)
        return stick_breaking(q_i, k, v, bias, q_pos, k_pos)

    o = lax.map(block, (jnp.arange(n_blk), q_blocks))
    return jnp.moveaxis(o, 0, 1).reshape(b, s, h, d)


def sb_sample(q, k_new, v_new, bias, pool_k, pool_v, page_table, layer):
    t = q.shape[1]
    past = page_table.shape[1] * PAGE_SIZE
    q_pos = past + jnp.arange(t)
    k_pos = jnp.arange(past + t)

    def one(args):
        q_b, k_b, v_b, pages = args
        k_all = jnp.concatenate([pool_k[layer, pages].reshape(past, SB_HEADS, HEAD_DIM).astype(k_b.dtype), k_b], axis=0)
        v_all = jnp.concatenate([pool_v[layer, pages].reshape(past, SB_HEADS, HEAD_DIM).astype(v_b.dtype), v_b], axis=0)
        return stick_breaking(q_b[None], k_all[None], v_all[None], bias, q_pos, k_pos)[0]

    return lax.map(one, (q, k_new, v_new, page_table))


def causal_depthwise_conv(c, hist, w, b):
    full = jnp.concatenate([hist.astype(c.dtype), c], axis=1)
    y = lax.conv_general_dilated(full, w[:, None, :].astype(c.dtype), window_strides=(1,), padding='VALID',
                                 dimension_numbers=('NWC', 'WIO', 'NWC'), feature_group_count=CONV_CH)
    return y + b, full[:, full.shape[1] - (CONV_K - 1):]


def memory_kv(mem, g_mem, w_k, w_v, g_k):
    b, m, _ = mem.shape
    mn = rms_norm(mem, g_mem)
    k = rms_norm((mn @ w_k).reshape(b, m, MEM_HEADS, HEAD_DIM), g_k)
    v = (mn @ w_v).reshape(b, m, MEM_HEADS, HEAD_DIM)
    return k, v


def memory_attend(q, k, v):
    s = jnp.einsum('bqhd,bmhd->bhqm', q.astype(jnp.float32), k.astype(jnp.float32)) * (HEAD_DIM ** -0.5)
    p = jax.nn.softmax(s, axis=-1)
    return jnp.einsum('bhqm,bmhd->bqhd', p, v.astype(jnp.float32)).astype(q.dtype)


def mixing_sublayer(x, lw, sb_fn, conv_hist, mem_k, mem_v):
    b, t, _ = x.shape
    h = rms_norm(x, lw['g_mix'])
    z = h @ lw['w_in']
    q_sb = z[..., :OFF_K].reshape(b, t, SB_HEADS, HEAD_DIM)
    k_sb = z[..., OFF_K:OFF_V].reshape(b, t, SB_HEADS, HEAD_DIM)
    v_sb = z[..., OFF_V:OFF_GLU_A].reshape(b, t, SB_HEADS, HEAD_DIM)
    glu = z[..., OFF_GLU_A:OFF_GLU_B] * jax.nn.sigmoid(z[..., OFF_GLU_B:OFF_MEM_Q])
    q_m = z[..., OFF_MEM_Q:].reshape(b, t, MEM_HEADS, HEAD_DIM)
    o_sb = rms_norm(sb_fn(q_sb, k_sb, v_sb, lw['sb_bias']), lw['sb_out_g'])
    c, new_hist = causal_depthwise_conv(glu, conv_hist, lw['conv_w'], lw['conv_b'])
    o_conv = jax.nn.silu(layer_norm(c, lw['conv_ln_g'], lw['conv_ln_b']))
    o_mem = rms_norm(memory_attend(rms_norm(q_m, lw['mem_q_g']), mem_k, mem_v), lw['mem_out_g'])
    merged = jnp.concatenate([o_sb.reshape(b, t, SB_WIDTH), o_conv, o_mem.reshape(b, t, MEM_WIDTH)], axis=-1)
    return x + merged @ lw['w_out'], k_sb, v_sb, new_hist


def peer_ffn(h, w_q, sub_keys, u, v):
    b, t, d = h.shape
    flat = h.reshape(b * t, d)
    n = flat.shape[0]
    flat = jnp.pad(flat, ((0, (-n) % PEER_CHUNK), (0, 0)))
    chunks = flat.reshape(-1, PEER_CHUNK, d)

    def one(xc):
        q = (xc @ w_q).reshape(PEER_CHUNK, PEER_HEADS, 2, PEER_DHALF)
        s = jnp.einsum('thcd,hcnd->thcn', q.astype(jnp.float32), sub_keys.astype(jnp.float32))
        s_top, i_top = lax.top_k(s, PEER_TOPK)
        cand = (s_top[:, :, 0, :, None] + s_top[:, :, 1, None, :]).reshape(PEER_CHUNK, PEER_HEADS, PEER_TOPK * PEER_TOPK)
        cand_id = (i_top[:, :, 0, :, None] * N_KEYS + i_top[:, :, 1, None, :]).reshape(PEER_CHUNK, PEER_HEADS, PEER_TOPK * PEER_TOPK)
        best, pos = lax.top_k(cand, PEER_TOPK)
        ids = jnp.take_along_axis(cand_id, pos, axis=-1)
        g = jax.nn.softmax(best, axis=-1)
        u_sel = u[ids]
        act = jax.nn.gelu(jnp.einsum('thkd,td->thk', u_sel, xc).astype(jnp.float32), approximate=False)
        return jnp.einsum('thk,thkd->td', (g * act).astype(xc.dtype), v[ids])

    y = lax.map(one, chunks).reshape(-1, d)[:n]
    return y.reshape(b, t, d)


def setup_inputs(seed: int = 0) -> dict:
    key = jax.random.key(seed)
    ks = jax.random.split(key, 32)
    f32 = jnp.float32
    n_pages = PAST_LEN // PAGE_SIZE
    n_used = DEC_BATCH * n_pages
    n_pool = n_used + n_used // 4 + 1

    def nrm(k, shape, scale=1.0):
        return scale * jax.random.normal(k, shape, f32)

    def gain(k, shape):
        return 1.0 + 0.02 * jax.random.normal(k, shape, f32)

    page_table = jax.random.permutation(ks[0], n_pool)[:n_used].reshape(DEC_BATCH, n_pages).astype(jnp.int32)
    return {
        'x_prompt': nrm(ks[1], (BATCH, SEQ, D_MODEL)),
        'x_sample': nrm(ks[2], (DEC_BATCH, DEC_SEQ, D_MODEL)),
        'mem_prompt': nrm(ks[3], (BATCH, MEM_TOKENS, D_MODEL)),
        'cache_sb_k': nrm(ks[4], (DEPTH, n_pool, PAGE_SIZE, SB_HEADS, HEAD_DIM)),
        'cache_sb_v': nrm(ks[5], (DEPTH, n_pool, PAGE_SIZE, SB_HEADS, HEAD_DIM)),
        'page_table': page_table,
        'cache_mem_k': nrm(ks[6], (DEPTH, DEC_BATCH, MEM_TOKENS, MEM_HEADS, HEAD_DIM)),
        'cache_mem_v': nrm(ks[7], (DEPTH, DEC_BATCH, MEM_TOKENS, MEM_HEADS, HEAD_DIM)),
        'state_conv': nrm(ks[8], (DEPTH, DEC_BATCH, CONV_K - 1, CONV_CH), 0.5),
        'g_mix': gain(ks[9], (DEPTH, D_MODEL)),
        'w_in': nrm(ks[10], (DEPTH, D_MODEL, IN_COLS), D_MODEL ** -0.5),
        'sb_bias': SB_LOGIT_OFFSET + 0.1 * jax.random.normal(ks[28], (DEPTH, SB_HEADS), f32),
        'sb_out_g': gain(ks[11], (DEPTH, SB_HEADS, HEAD_DIM)),
        'conv_w': nrm(ks[12], (DEPTH, CONV_K, CONV_CH), CONV_K ** -0.5),
        'conv_b': nrm(ks[13], (DEPTH, CONV_CH), 0.02),
        'conv_ln_g': gain(ks[14], (DEPTH, CONV_CH)),
        'conv_ln_b': nrm(ks[15], (DEPTH, CONV_CH), 0.02),
        'g_mem': gain(ks[16], (DEPTH, D_MODEL)),
        'w_mem_k': nrm(ks[17], (DEPTH, D_MODEL, MEM_WIDTH), D_MODEL ** -0.5),
        'w_mem_v': nrm(ks[18], (DEPTH, D_MODEL, MEM_WIDTH), D_MODEL ** -0.5),
        'mem_q_g': gain(ks[19], (DEPTH, HEAD_DIM)),
        'mem_k_g': gain(ks[20], (DEPTH, HEAD_DIM)),
        'mem_out_g': gain(ks[21], (DEPTH, MEM_HEADS, HEAD_DIM)),
        'w_out': nrm(ks[22], (DEPTH, MIX_WIDTH, D_MODEL), MIX_WIDTH ** -0.5),
        'g_ffn': gain(ks[23], (DEPTH, D_MODEL)),
        'peer_w_q': nrm(ks[24], (DEPTH, D_MODEL, PEER_HEADS * PEER_DKEY), D_MODEL ** -0.5),
        'peer_sub_keys': nrm(ks[25], (DEPTH, PEER_HEADS, 2, N_KEYS, PEER_DHALF), PEER_DHALF ** -0.5),
        'peer_u': nrm(ks[26], (DEPTH, N_EXPERTS, D_MODEL), D_MODEL ** -0.5),
        'peer_v': nrm(ks[27], (DEPTH, N_EXPERTS, D_MODEL), PEER_HEADS ** -0.5),
    }


def reference(x_prompt, x_sample, mem_prompt, cache_sb_k, cache_sb_v, page_table, cache_mem_k, cache_mem_v,
              state_conv, g_mix, w_in, sb_bias, sb_out_g, conv_w, conv_b, conv_ln_g, conv_ln_b, g_mem, w_mem_k,
              w_mem_v, mem_q_g, mem_k_g, mem_out_g, w_out, g_ffn, peer_w_q, peer_sub_keys, peer_u, peer_v):
    xp, xs = x_prompt, x_sample
    kp_l, vp_l, ks_l, vs_l, mk_l, mv_l, cp_l, cs_l = [], [], [], [], [], [], [], []
    for l in range(DEPTH):
        lw = {'g_mix': g_mix[l], 'w_in': w_in[l], 'sb_bias': sb_bias[l], 'sb_out_g': sb_out_g[l],
              'conv_w': conv_w[l], 'conv_b': conv_b[l], 'conv_ln_g': conv_ln_g[l], 'conv_ln_b': conv_ln_b[l],
              'mem_q_g': mem_q_g[l], 'mem_out_g': mem_out_g[l], 'w_out': w_out[l]}
        mk_p, mv_p = memory_kv(mem_prompt, g_mem[l], w_mem_k[l], w_mem_v[l], mem_k_g[l])
        hist0 = jnp.zeros((xp.shape[0], CONV_K - 1, CONV_CH), xp.dtype)
        xp, kp, vp, cp = mixing_sublayer(xp, lw, sb_prompt, hist0, mk_p, mv_p)
        sb_fn = functools.partial(sb_sample, pool_k=cache_sb_k, pool_v=cache_sb_v, page_table=page_table, layer=l)
        xs, ks_, vs_, cs_ = mixing_sublayer(xs, lw, sb_fn, state_conv[l], cache_mem_k[l], cache_mem_v[l])
        xp = xp + peer_ffn(rms_norm(xp, g_ffn[l]), peer_w_q[l], peer_sub_keys[l], peer_u[l], peer_v[l])
        xs = xs + peer_ffn(rms_norm(xs, g_ffn[l]), peer_w_q[l], peer_sub_keys[l], peer_u[l], peer_v[l])
        kp_l.append(kp); vp_l.append(vp); ks_l.append(ks_); vs_l.append(vs_)
        mk_l.append(mk_p); mv_l.append(mv_p); cp_l.append(cp); cs_l.append(cs_)
    return (xp, xs, jnp.stack(kp_l), jnp.stack(vp_l), jnp.stack(ks_l), jnp.stack(vs_l),
            jnp.stack(mk_l), jnp.stack(mv_l), jnp.stack(cp_l), jnp.stack(cs_l))
```

```python
import functools
import math

import jax
import jax.numpy as jnp
from jax import lax
from jax.experimental import pallas as pl
from jax.experimental.pallas import tpu as pltpu

F32 = jnp.float32
BF16 = jnp.bfloat16

EPS = 1e-6
HEAD_DIM = 64
LANES = 128
TOPK = 16
N_KEYS = 128
CONV_PAD = 32
VMEM_LIMIT = 56 * 1024 * 1024
NEG_INF = float("-inf")
POS_INF = float("inf")
INV_SQRT2 = 1.0 / math.sqrt(2.0)


def _nt(a, b):
    return lax.dot_general(a, b, (((1,), (1,)), ((), ())), preferred_element_type=F32)


def _mm(a, b):
    return jnp.dot(a, b, preferred_element_type=F32)


def _group_mean_sq(x, bd):
    sq = x * x
    hi = sq.astype(BF16)
    lo = (sq - hi.astype(F32)).astype(BF16)
    return _mm(hi, bd) + _mm(lo, bd)


def _params(*sem):
    return pltpu.CompilerParams(dimension_semantics=sem, vmem_limit_bytes=VMEM_LIMIT)


def _inproj_kernel(x_ref, g_ref, w_ref, bd_ref, gq_ref,
                   q_ref, k_ref, v_ref, kb_ref, vb_ref, glu_ref, qm_ref, *, sbw, cch):
    x = x_ref[...]
    xn = x * lax.rsqrt(jnp.mean(x * x, axis=-1, keepdims=True) + EPS) * g_ref[...]
    z = _mm(xn.astype(BF16), w_ref[...])
    q_ref[...] = (z[:, :sbw] * (HEAD_DIM ** -0.5)).astype(BF16)
    k = z[:, sbw:2 * sbw]
    v = z[:, 2 * sbw:3 * sbw]
    k_ref[...] = k
    v_ref[...] = v
    kb_ref[...] = k.astype(BF16)
    vb_ref[...] = v.astype(BF16)
    o = 3 * sbw
    glu_ref[...] = z[:, o:o + cch] * jax.nn.sigmoid(z[:, o + cch:o + 2 * cch])
    qm = z[:, o + 2 * cch:]
    qm_ref[...] = (qm * lax.rsqrt(_group_mean_sq(qm, bd_ref[...]) + EPS) * gq_ref[...]).astype(BF16)


def _inproj(x, g, w_bf, bd_mem, gq, *, sbw, cch, memw, tm):
    t, d = x.shape
    cols = w_bf.shape[1]
    row = lambda width: pl.BlockSpec((tm, width), lambda i: (i, 0))
    full = lambda a: pl.BlockSpec(a.shape, lambda i: (0,) * a.ndim)
    return pl.pallas_call(
        functools.partial(_inproj_kernel, sbw=sbw, cch=cch),
        grid=(t // tm,),
        in_specs=[row(d), full(g), full(w_bf), full(bd_mem), full(gq)],
        out_specs=[row(sbw), row(sbw), row(sbw), row(sbw), row(sbw), row(cch), row(memw)],
        out_shape=[jax.ShapeDtypeStruct((t, sbw), BF16), jax.ShapeDtypeStruct((t, sbw), F32),
                   jax.ShapeDtypeStruct((t, sbw), F32), jax.ShapeDtypeStruct((t, sbw), BF16),
                   jax.ShapeDtypeStruct((t, sbw), BF16), jax.ShapeDtypeStruct((t, cch), F32),
                   jax.ShapeDtypeStruct((t, memw), BF16)],
        compiler_params=_params("parallel"),
        name="inproj",
    )(x, g, w_bf, bd_mem, gq)


def _sb_block(qh, bias, kblk, vblk, tri, acc, car, mask):
    bk = kblk.shape[0]
    z = _nt(qh, kblk) + bias
    sp = jnp.log1p(jnp.exp(-jnp.abs(z)))
    lb = jnp.minimum(z, 0.0) - sp
    l1m = lb - z
    if mask is not None:
        l1m = jnp.where(mask, l1m, 0.0)
    r = _mm(l1m.astype(BF16), tri)
    w = jnp.exp(lb + r[:, :bk] + car)
    if mask is not None:
        w = jnp.where(mask, w, 0.0)
    return acc + _mm(w.astype(BF16), vblk), car + r[:, bk:]


def _sb_prompt_kernel(bias_ref, q_ref, k_ref, v_ref, tri_ref, g_ref, o_ref,
                      acc_a, car_a, acc_b, car_b, *, bq, bk):
    hp = pl.program_id(1)
    qi = pl.program_id(2)
    q2 = q_ref[...]
    lane = lax.broadcasted_iota(jnp.int32, (bq, LANES), 1)
    first = lane < HEAD_DIM
    zero = jnp.zeros_like(q2)
    qa = jnp.where(first, q2, zero)
    qb = jnp.where(first, zero, q2)
    bias_a = bias_ref[2 * hp]
    bias_b = bias_ref[2 * hp + 1]
    tri = tri_ref[...]
    for r in (acc_a, car_a, acc_b, car_b):
        r[...] = jnp.zeros_like(r)

    def step(kb, mask):
        off = pl.multiple_of(kb * bk, bk)
        kblk = k_ref[pl.ds(off, bk), :]
        vblk = v_ref[pl.ds(off, bk), :]
        acc_a[...], car_a[...] = _sb_block(qa, bias_a, kblk, vblk, tri, acc_a[...], car_a[...], mask)
        acc_b[...], car_b[...] = _sb_block(qb, bias_b, kblk, vblk, tri, acc_b[...], car_b[...], mask)

    nd = bq // bk
    rows = lax.broadcasted_iota(jnp.int32, (bq, bk), 0)
    cols = lax.broadcasted_iota(jnp.int32, (bq, bk), 1)
    for d in reversed(range(nd)):
        step(qi * nd + d, (cols + d * bk) < rows)

    def body(i, c):
        step(qi * nd - 1 - i, None)
        return c

    lax.fori_loop(0, qi * nd, body, 0)

    o = jnp.where(first, acc_a[...], acc_b[...])
    sq = o * o
    ms_a = jnp.sum(jnp.where(first, sq, 0.0), axis=-1, keepdims=True)
    ms_b = jnp.sum(jnp.where(first, 0.0, sq), axis=-1, keepdims=True)
    ms = jnp.where(first, ms_a, ms_b) * (1.0 / HEAD_DIM)
    o_ref[...] = (o * lax.rsqrt(ms + EPS) * g_ref[...]).astype(BF16)


def _tri_matrix(bk):
    j = jnp.arange(bk)[:, None]
    s = jnp.arange(bk)[None, :]
    return jnp.concatenate([(j > s).astype(BF16), jnp.ones((bk, LANES), BF16)], axis=1)


def _sb_prompt(q_bf, k_bf, v_bf, bias, g_row, *, batch, seq, bq, bk):
    t, sbw = q_bf.shape
    npair = sbw // LANES
    nq = seq // bq
    return pl.pallas_call(
        functools.partial(_sb_prompt_kernel, bq=bq, bk=bk),
        grid=(batch, npair, nq),
        in_specs=[pl.BlockSpec(memory_space=pltpu.SMEM),
                  pl.BlockSpec((bq, LANES), lambda b, p, i: (b * nq + i, p)),
                  pl.BlockSpec((seq, LANES), lambda b, p, i: (b, p)),
                  pl.BlockSpec((seq, LANES), lambda b, p, i: (b, p)),
                  pl.BlockSpec((bk, bk + LANES), lambda b, p, i: (0, 0)),
                  pl.BlockSpec((1, LANES), lambda b, p, i: (0, p))],
        out_specs=pl.BlockSpec((bq, LANES), lambda b, p, i: (b * nq + i, p)),
        out_shape=jax.ShapeDtypeStruct((t, sbw), BF16),
        scratch_shapes=[pltpu.VMEM((bq, LANES), F32)] * 4,
        compiler_params=_params("parallel", "parallel", "arbitrary"),
        name="sb_prompt",
    )(bias, q_bf, k_bf, v_bf, _tri_matrix(bk), g_row)


def _sb_sample_kernel(pt_ref, q_ref, kn_ref, vn_ref, bias_ref, tri_ref, bd_ref, g_ref, *rest,
                      pages, n_heads, n_new):
    k_refs = rest[:pages]
    v_refs = rest[pages:2 * pages]
    o_ref = rest[2 * pages]
    acc, car = rest[2 * pages + 1:]
    s = pl.program_id(1)
    q = q_ref[0]
    bias = bias_ref[...]
    tri = tri_ref[...]
    bk = tri.shape[0]
    nrow, width = q.shape

    @pl.when(s == 0)
    def _():
        pad = jnp.zeros((bk - kn_ref.shape[1], width), BF16)
        kblk = jnp.concatenate([kn_ref[0], pad], axis=0)
        vblk = jnp.concatenate([vn_ref[0], pad], axis=0)
        rows = lax.broadcasted_iota(jnp.int32, (nrow, bk), 0)
        cols = lax.broadcasted_iota(jnp.int32, (nrow, bk), 1)
        mask = cols < rows // n_heads
        acc[...], car[...] = _sb_block(q, bias, kblk, vblk, tri, jnp.zeros(acc.shape, F32),
                                       jnp.zeros(car.shape, F32), mask)

    for r in range(pages):
        acc[...], car[...] = _sb_block(q, bias, k_refs[r][0].astype(BF16), v_refs[r][0].astype(BF16),
                                       tri, acc[...], car[...], None)

    @pl.when(s == pl.num_programs(1) - 1)
    def _():
        rows = lax.broadcasted_iota(jnp.int32, (nrow, width), 0)
        cols = lax.broadcasted_iota(jnp.int32, (nrow, width), 1)
        own = jnp.where(cols // HEAD_DIM == rows % n_heads, acc[...], 0.0)
        o = jnp.sum(own.reshape(n_new, n_heads, width), axis=1)
        o = o * lax.rsqrt(_group_mean_sq(o, bd_ref[...]) + EPS) * g_ref[...]
        o_ref[0] = o.astype(BF16)


def _sb_sample(q_bf, kn_bf, vn_bf, bias, g_row, bd_sb, pool_k, pool_v, page_table, *, n_new, pages):
    nb, npg = page_table.shape
    tok, width = pool_k.shape[1:]
    n_heads = width // HEAD_DIM
    nrow = n_new * n_heads
    nstep = npg // pages
    q3 = q_bf.reshape(nb, n_new, width)
    head_of_lane = jnp.arange(width) // HEAD_DIM
    qbd = jnp.where(head_of_lane[None, None, None, :] == jnp.arange(n_heads)[None, None, :, None],
                    q3[:, :, None, :], jnp.zeros((), BF16)).reshape(nb, nrow, width)
    pad_new = 8 - n_new
    kn = jnp.pad(kn_bf.reshape(nb, n_new, width), ((0, 0), (0, pad_new), (0, 0)))
    vn = jnp.pad(vn_bf.reshape(nb, n_new, width), ((0, 0), (0, pad_new), (0, 0)))
    bias_rows = jnp.broadcast_to(jnp.tile(bias, n_new)[:, None], (nrow, LANES)).astype(F32)

    def page_spec(r):
        return pl.BlockSpec((1, tok, width), lambda b, s, pt: (pt[b, npg - 1 - (s * pages + r)], 0, 0))

    const = lambda shape: pl.BlockSpec(shape, lambda b, s, pt: (0,) * len(shape))
    per_b = lambda shape: pl.BlockSpec(shape, lambda b, s, pt: (b,) + (0,) * (len(shape) - 1))
    grid_spec = pltpu.PrefetchScalarGridSpec(
        num_scalar_prefetch=1,
        grid=(nb, nstep),
        in_specs=[per_b((1, nrow, width)), per_b((1, 8, width)), per_b((1, 8, width)),
                  const((nrow, LANES)), const((tok, tok + LANES)), const((width, width)), const((1, width))]
                 + [page_spec(r) for r in range(pages)] * 2,
        out_specs=per_b((1, n_new, width)),
        scratch_shapes=[pltpu.VMEM((nrow, width), F32), pltpu.VMEM((nrow, LANES), F32)],
    )
    return pl.pallas_call(
        functools.partial(_sb_sample_kernel, pages=pages, n_heads=n_heads, n_new=n_new),
        grid_spec=grid_spec,
        out_shape=jax.ShapeDtypeStruct((nb, n_new, width), BF16),
        compiler_params=_params("parallel", "arbitrary"),
        name="sb_sample",
    )(page_table, qbd, kn, vn, bias_rows, _tri_matrix(tok), bd_sb, g_row,
      *([pool_k] * pages), *([pool_v] * pages))


def _conv_kernel(*refs, ts, taps, chunk, has_prev):
    if has_prev:
        hist_ref, prev_ref, cur_ref, w_ref, b_ref, lg_ref, lb_ref, o_ref, buf = refs
        head = jnp.where(pl.program_id(1) == 0, hist_ref[0], prev_ref[0])
    else:
        hist_ref, cur_ref, w_ref, b_ref, lg_ref, lb_ref, o_ref, buf = refs
        head = hist_ref[0]
    buf[0:CONV_PAD, :] = head
    buf[CONV_PAD:CONV_PAD + ts, :] = cur_ref[0]
    first = CONV_PAD - (taps - 1)
    for c0 in range(0, ts, chunk):
        acc = jnp.zeros((chunk, buf.shape[1]), F32)
        for k in range(taps):
            acc = acc + w_ref[k:k + 1, :] * buf[first + c0 + k:first + c0 + k + chunk, :]
        c = acc + b_ref[...]
        xc = c - jnp.mean(c, axis=-1, keepdims=True)
        y = xc * lax.rsqrt(jnp.mean(xc * xc, axis=-1, keepdims=True) + EPS) * lg_ref[...] + lb_ref[...]
        o_ref[0, c0:c0 + chunk, :] = (y * jax.nn.sigmoid(y)).astype(BF16)


def _conv_module(glu3, hist_pad, w, b, lg, lb, *, ts):
    nb, s, cch = glu3.shape
    taps = w.shape[0]
    has_prev = s > ts
    chunk = min(ts, 64)
    cur = pl.BlockSpec((1, ts, cch), lambda bb, i: (bb, i, 0))
    hist = pl.BlockSpec((1, CONV_PAD, cch), lambda bb, i: (bb, 0, 0))
    const = lambda a: pl.BlockSpec(a.shape, lambda bb, i: (0,) * a.ndim)
    in_specs, args = [hist], [hist_pad]
    if has_prev:
        per = ts // CONV_PAD
        in_specs.append(pl.BlockSpec((1, CONV_PAD, cch), lambda bb, i: (bb, jnp.maximum(i * per - 1, 0), 0)))
        args.append(glu3)
    in_specs += [cur, const(w), const(b), const(lg), const(lb)]
    args += [glu3, w, b, lg, lb]
    return pl.pallas_call(
        functools.partial(_conv_kernel, ts=ts, taps=taps, chunk=chunk, has_prev=has_prev),
        grid=(nb, s // ts),
        in_specs=in_specs,
        out_specs=cur,
        out_shape=jax.ShapeDtypeStruct((nb, s, cch), BF16),
        scratch_shapes=[pltpu.VMEM((CONV_PAD + max(ts, 8), cch), F32)],
        compiler_params=_params("parallel", "parallel"),
        name="conv_module",
    )(*args)


def _memkv_kernel(m_ref, g_ref, wk_ref, wv_ref, bd_ref, gk_ref, k_ref, v_ref):
    x = m_ref[...]
    xn = (x * lax.rsqrt(jnp.mean(x * x, axis=-1, keepdims=True) + EPS) * g_ref[...]).astype(BF16)
    k = _mm(xn, wk_ref[...])
    k_ref[...] = k * lax.rsqrt(_group_mean_sq(k, bd_ref[...]) + EPS) * gk_ref[...]
    v_ref[...] = _mm(xn, wv_ref[...])


def _memkv(mem, g, wk_bf, wv_bf, bd_mem, gk, *, tm):
    t, d = mem.shape
    memw = wk_bf.shape[1]
    full = lambda a: pl.BlockSpec(a.shape, lambda i: (0,) * a.ndim)
    out = pl.BlockSpec((tm, memw), lambda i: (i, 0))
    return pl.pallas_call(
        _memkv_kernel,
        grid=(t // tm,),
        in_specs=[pl.BlockSpec((tm, d), lambda i: (i, 0)), full(g), full(wk_bf), full(wv_bf), full(bd_mem), full(gk)],
        out_specs=[out, out],
        out_shape=[jax.ShapeDtypeStruct((t, memw), F32)] * 2,
        compiler_params=_params("parallel"),
        name="memkv",
    )(mem, g, wk_bf, wv_bf, bd_mem, gk)


def _mem_attn_kernel(q_ref, k_ref, v_ref, bd_ref, g_ref, o_ref, *, n_heads):
    q = q_ref[0]
    tq = q.shape[0]
    rows = max(tq, 8)
    if rows != tq:
        q = jnp.concatenate([q, jnp.zeros((rows - tq, q.shape[1]), q.dtype)], axis=0)
    kb = k_ref[0].astype(BF16)
    vb = v_ref[0].astype(BF16)
    head = lax.broadcasted_iota(jnp.int32, q.shape, 1) // HEAD_DIM
    zero = jnp.zeros_like(q)
    out = jnp.zeros(q.shape, F32)
    for h in range(n_heads):
        s = _nt(jnp.where(head == h, q, zero), kb) * (HEAD_DIM ** -0.5)
        p = jnp.exp(s - jnp.max(s, axis=-1, keepdims=True))
        oh = _mm(p.astype(BF16), vb) / jnp.sum(p, axis=-1, keepdims=True)
        out = jnp.where(head == h, oh, out)
    out = out * lax.rsqrt(_group_mean_sq(out, bd_ref[...]) + EPS) * g_ref[...]
    o_ref[0] = out[:tq].astype(BF16)


def _mem_attn(q3, k3, v3, bd_mem, g_row, *, tq):
    ng, s, memw = q3.shape
    m = k3.shape[1]
    const = lambda a: pl.BlockSpec(a.shape, lambda gq, i: (0,) * a.ndim)
    qspec = pl.BlockSpec((1, tq, memw), lambda gq, i: (gq, i, 0))
    kspec = pl.BlockSpec((1, m, memw), lambda gq, i: (gq, 0, 0))
    return pl.pallas_call(
        functools.partial(_mem_attn_kernel, n_heads=memw // HEAD_DIM),
        grid=(ng, s // tq),
        in_specs=[qspec, kspec, kspec, const(bd_mem), const(g_row)],
        out_specs=qspec,
        out_shape=jax.ShapeDtypeStruct((ng, s, memw), BF16),
        compiler_params=_params("parallel", "parallel"),
        name="mem_attn",
    )(q3, k3, v3, bd_mem, g_row)


def _outproj_kernel(x_ref, sb_ref, cv_ref, mm_ref, w_ref, o_ref):
    a = sb_ref.shape[1]
    b = a + cv_ref.shape[1]
    y = _mm(sb_ref[...], w_ref[0:a, :]) + _mm(cv_ref[...], w_ref[a:b, :]) + _mm(mm_ref[...], w_ref[b:, :])
    o_ref[...] = x_ref[...] + y


def _outproj(x, o_sb, o_conv, o_mem, w_bf, *, tm):
    t, d = x.shape
    row = lambda a: pl.BlockSpec((tm, a.shape[1]), lambda i: (i, 0))
    return pl.pallas_call(
        _outproj_kernel,
        grid=(t // tm,),
        in_specs=[row(x), row(o_sb), row(o_conv), row(o_mem), pl.BlockSpec(w_bf.shape, lambda i: (0, 0))],
        out_specs=row(x),
        out_shape=jax.ShapeDtypeStruct((t, d), F32),
        compiler_params=_params("parallel"),
        name="outproj",
    )(x, o_sb, o_conv, o_mem, w_bf)


def _top_values(s, n):
    tops = []
    cur = s
    for _ in range(n):
        m = jnp.max(cur, axis=0, keepdims=True)
        tops.append(m)
        cur = jnp.where(cur >= m, NEG_INF, cur)
    return tops


def _peer_select_tile(s1, s2):
    t1 = jnp.concatenate(_top_values(s1, TOPK), axis=0)
    t2 = jnp.concatenate(_top_values(s2, TOPK), axis=0)
    row8 = lax.broadcasted_iota(jnp.int32, (8, LANES), 0)
    parts = [t1[0:1] + t2]
    for a in range(1, 8):
        parts.append(jnp.where(row8 < TOPK // (a + 1), t1[a:a + 1] + t2[0:8], NEG_INF))
    parts.append(t1[8:TOPK] + t2[0:1])
    best = _top_values(jnp.concatenate(parts, axis=0), TOPK)
    tau = best[TOPK - 1]
    top = best[0]
    zsum = jnp.zeros_like(top)
    for bv in best:
        zsum = zsum + jnp.exp(bv - top)
    th = jnp.full(s1.shape, POS_INF, F32)
    for a in range(TOPK):
        sel = (t1[a:a + 1] + t2) >= tau
        th_a = jnp.min(jnp.where(sel, t2, POS_INF), axis=0, keepdims=True)
        th = jnp.where(s1 == t1[a:a + 1], th_a, th)
    e1 = jnp.exp(s1 - t1[0:1]) / zsum
    e2 = jnp.exp(s2 - t2[0:1])
    return th, e1, e2


def _peer_select_kernel(x_ref, g_ref, wq_ref, keys_ref, xn_ref, th_ref, e1_ref, s2_ref, e2_ref,
                        qt_scr, s1_scr, s2_scr, *, n_heads):
    x = x_ref[...]
    xn = (x * lax.rsqrt(jnp.mean(x * x, axis=-1, keepdims=True) + EPS) * g_ref[...]).astype(BF16)
    xn_ref[...] = xn
    qt_scr[...] = _nt(wq_ref[...], xn).astype(BF16)
    dh = keys_ref.shape[2]
    groups = x.shape[0] // LANES

    def head(h, c):
        r1 = pl.multiple_of(h * 2 * dh, 2 * dh)
        r2 = pl.multiple_of(h * 2 * dh + dh, dh)
        s1_scr[...] = _mm(keys_ref[2 * h], qt_scr[pl.ds(r1, dh), :])
        s2_scr[...] = _mm(keys_ref[2 * h + 1], qt_scr[pl.ds(r2, dh), :])
        for gi in range(groups):
            s2 = s2_scr[:, gi * LANES:(gi + 1) * LANES]
            th, e1, e2 = _peer_select_tile(s1_scr[:, gi * LANES:(gi + 1) * LANES], s2)
            th_ref[h, gi] = th
            e1_ref[h, gi] = e1
            s2_ref[h, gi] = s2
            e2_ref[h, gi] = e2
        return c

    lax.fori_loop(0, n_heads, head, 0)


def _peer_select(x, g, wq_t_bf, keys_bf, *, tm):
    t, d = x.shape
    n_heads = keys_bf.shape[0] // 2
    groups = tm // LANES
    full = lambda a: pl.BlockSpec(a.shape, lambda i: (0,) * a.ndim)
    sel = pl.BlockSpec((n_heads, groups, N_KEYS, LANES), lambda i: (0, i, 0, 0))
    sel_shape = jax.ShapeDtypeStruct((n_heads, t // LANES, N_KEYS, LANES), F32)
    return pl.pallas_call(
        functools.partial(_peer_select_kernel, n_heads=n_heads),
        grid=(t // tm,),
        in_specs=[pl.BlockSpec((tm, d), lambda i: (i, 0)), full(g), full(wq_t_bf), full(keys_bf)],
        out_specs=[pl.BlockSpec((tm, d), lambda i: (i, 0)), sel, sel, sel, sel],
        out_shape=[jax.ShapeDtypeStruct((t, d), BF16), sel_shape, sel_shape, sel_shape, sel_shape],
        scratch_shapes=[pltpu.VMEM((wq_t_bf.shape[0], tm), BF16), pltpu.VMEM((N_KEYS, tm), F32),
                        pltpu.VMEM((N_KEYS, tm), F32)],
        compiler_params=_params("parallel"),
        name="peer_select",
    )(x, g, wq_t_bf, keys_bf)


def _peer_dense_kernel(xn_ref, u_ref, vt_ref, th_ref, e1_ref, s2_ref, e2_ref, xmid_ref, o_ref,
                       a_scr, wa_scr, acc_scr, *, chunk):
    eb = pl.program_id(1)
    n_heads, groups = th_ref.shape[:2]
    rows_i = u_ref.shape[0] // N_KEYS
    per_i = N_KEYS // chunk

    @pl.when(eb == 0)
    def _():
        acc_scr[...] = jnp.zeros_like(acc_scr)

    a_scr[...] = _nt(u_ref[...], xn_ref[...])
    for gi in range(groups):
        lanes = slice(gi * LANES, (gi + 1) * LANES)

        def body(it, c):
            i_row = eb * rows_i + it // per_i
            j0 = pl.multiple_of((it % per_i) * chunk, chunk)
            r0 = pl.multiple_of(it * chunk, chunk)
            a = a_scr[pl.ds(r0, chunk), lanes]
            act = 0.5 * a * (1.0 + lax.erf(a * INV_SQRT2))
            gate = jnp.zeros((chunk, LANES), F32)
            for h in range(n_heads):
                th = th_ref[h, gi, pl.ds(i_row, 1), :]
                e1 = e1_ref[h, gi, pl.ds(i_row, 1), :]
                s2 = s2_ref[h, gi, pl.ds(j0, chunk), :]
                e2 = e2_ref[h, gi, pl.ds(j0, chunk), :]
                gate = gate + jnp.where(s2 >= th, e2, 0.0) * e1
            wa_scr[pl.ds(r0, chunk), lanes] = (gate * act).astype(BF16)
            return c

        lax.fori_loop(0, rows_i * per_i, body, 0)

    acc_scr[...] += _mm(vt_ref[...], wa_scr[...])

    @pl.when(eb == pl.num_programs(1) - 1)
    def _():
        o_ref[...] = xmid_ref[...] + acc_scr[...].T


def _peer_dense(xn_bf, u_bf, vt_bf, th, e1, s2, e2, x_mid, *, tm, eb):
    t, d = x_mid.shape
    n_exp = u_bf.shape[0]
    n_heads = th.shape[0]
    groups = tm // LANES
    tok = pl.BlockSpec((tm, d), lambda i, e: (i, 0))
    sel = pl.BlockSpec((n_heads, groups, N_KEYS, LANES), lambda i, e: (0, i, 0, 0))
    return pl.pallas_call(
        functools.partial(_peer_dense_kernel, chunk=64),
        grid=(t // tm, n_exp // eb),
        in_specs=[tok, pl.BlockSpec((eb, d), lambda i, e: (e, 0)), pl.BlockSpec((d, eb), lambda i, e: (0, e)),
                  sel, sel, sel, sel, tok],
        out_specs=tok,
        out_shape=jax.ShapeDtypeStruct((t, d), F32),
        scratch_shapes=[pltpu.VMEM((eb, tm), F32), pltpu.VMEM((eb, tm), BF16), pltpu.VMEM((d, tm), F32)],
        compiler_params=_params("parallel", "arbitrary"),
        name="peer_dense",
    )(xn_bf, u_bf, vt_bf, th, e1, s2, e2, x_mid)


def _block_diag(width):
    idx = jnp.arange(width) // HEAD_DIM
    return jnp.where(idx[:, None] == idx[None, :], 1.0 / HEAD_DIM, 0.0).astype(BF16)


def _tile(n, want):
    return want if n % want == 0 else n


def kernel(x_prompt, x_sample, mem_prompt, cache_sb_k, cache_sb_v, page_table, cache_mem_k, cache_mem_v,
           state_conv, g_mix, w_in, sb_bias, sb_out_g, conv_w, conv_b, conv_ln_g, conv_ln_b, g_mem, w_mem_k,
           w_mem_v, mem_q_g, mem_k_g, mem_out_g, w_out, g_ffn, peer_w_q, peer_sub_keys, peer_u, peer_v):
    depth = w_in.shape[0]
    batch, seq, d = x_prompt.shape
    nb, n_new, _ = x_sample.shape
    sb_heads = sb_bias.shape[1]
    sbw = sb_heads * HEAD_DIM
    cch = conv_w.shape[2]
    taps = conv_w.shape[1]
    mem_heads = mem_out_g.shape[1]
    memw = mem_heads * HEAD_DIM
    mem_tok = mem_prompt.shape[1]
    n_pool, page = cache_sb_k.shape[1:3]
    n_exp = peer_u.shape[1]

    bd_mem = _block_diag(memw)
    bd_sb = _block_diag(sbw)
    xp = x_prompt.reshape(batch * seq, d)
    xs = x_sample.reshape(nb * n_new, d)
    tp = _tile(batch * seq, 512)
    tsm = _tile(nb * n_new, 512)
    outs = [[] for _ in range(8)]

    for l in range(depth):
        row = lambda a: a[l].reshape(1, -1)
        w_in_bf = w_in[l].astype(BF16)
        w_out_bf = w_out[l].astype(BF16)
        gq = jnp.tile(mem_q_g[l], mem_heads).reshape(1, memw)
        gk = jnp.tile(mem_k_g[l], mem_heads).reshape(1, memw)
        conv_args = (conv_w[l], row(conv_b), row(conv_ln_g), row(conv_ln_b))

        mk, mv = _memkv(mem_prompt.reshape(batch * mem_tok, d), row(g_mem), w_mem_k[l].astype(BF16),
                        w_mem_v[l].astype(BF16), bd_mem, gk, tm=_tile(batch * mem_tok, 512))

        q, k, v, kb, vb, glu, qm = _inproj(xp, row(g_mix), w_in_bf, bd_mem, gq, sbw=sbw, cch=cch, memw=memw, tm=tp)
        o_sb = _sb_prompt(q, kb, vb, sb_bias[l], row(sb_out_g), batch=batch, seq=seq,
                          bq=_tile(seq, 256), bk=LANES)
        glu3 = glu.reshape(batch, seq, cch)
        o_conv = _conv_module(glu3, jnp.zeros((batch, CONV_PAD, cch), F32), *conv_args, ts=_tile(seq, 256))
        o_mem = _mem_attn(qm.reshape(batch, seq, memw), mk.reshape(batch, mem_tok, memw),
                          mv.reshape(batch, mem_tok, memw), bd_mem, row(mem_out_g), tq=_tile(seq, 512))
        xp_mid = _outproj(xp, o_sb, o_conv.reshape(-1, cch), o_mem.reshape(-1, memw), w_out_bf, tm=tp)

        q_s, k_s, v_s, kb_s, vb_s, glu_s, qm_s = _inproj(xs, row(g_mix), w_in_bf, bd_mem, gq,
                                                        sbw=sbw, cch=cch, memw=memw, tm=tsm)
        npg = page_table.shape[1]
        o_sb_s = _sb_sample(q_s, kb_s, vb_s, sb_bias[l], row(sb_out_g), bd_sb,
                            cache_sb_k[l].reshape(n_pool, page, sbw), cache_sb_v[l].reshape(n_pool, page, sbw),
                            page_table, n_new=n_new, pages=8 if npg % 8 == 0 else npg)
        glu_s3 = glu_s.reshape(nb, n_new, cch)
        hist_s = jnp.pad(state_conv[l], ((0, 0), (CONV_PAD - (taps - 1), 0), (0, 0)))
        o_conv_s = _conv_module(glu_s3, hist_s, *conv_args, ts=n_new)
        o_mem_s = _mem_attn(qm_s.reshape(nb, n_new, memw), cache_mem_k[l].reshape(nb, mem_tok, memw),
                            cache_mem_v[l].reshape(nb, mem_tok, memw), bd_mem, row(mem_out_g), tq=n_new)
        xs_mid = _outproj(xs, o_sb_s.reshape(-1, sbw), o_conv_s.reshape(-1, cch), o_mem_s.reshape(-1, memw),
                          w_out_bf, tm=tsm)

        wq_t = peer_w_q[l].T.astype(BF16)
        keys = peer_sub_keys[l].reshape(-1, N_KEYS, peer_sub_keys.shape[-1]).astype(BF16)
        u_bf = peer_u[l].astype(BF16)
        vt_bf = peer_v[l].T.astype(BF16)
        new = []
        for x_mid, tm in ((xp_mid, tp), (xs_mid, tsm)):
            xn, th, e1, s2, e2 = _peer_select(x_mid, row(g_ffn), wq_t, keys, tm=tm)
            new.append(_peer_dense(xn, u_bf, vt_bf, th, e1, s2, e2, x_mid, tm=tm, eb=_tile(n_exp, 1024)))
        xp, xs = new

        hist_rows = taps - 1
        cs = jnp.concatenate([state_conv[l], glu_s3], axis=1)[:, -hist_rows:]
        for lst, val in zip(outs, (k.reshape(batch, seq, sb_heads, HEAD_DIM), v.reshape(batch, seq, sb_heads, HEAD_DIM),
                                   k_s.reshape(nb, n_new, sb_heads, HEAD_DIM), v_s.reshape(nb, n_new, sb_heads, HEAD_DIM),
                                   mk.reshape(batch, mem_tok, mem_heads, HEAD_DIM),
                                   mv.reshape(batch, mem_tok, mem_heads, HEAD_DIM),
                                   glu3[:, seq - hist_rows:], cs)):
            lst.append(val)

    return (xp.reshape(batch, seq, d), xs.reshape(nb, n_new, d), *[jnp.stack(o) for o in outs])
```

```python
import functools
import math

import jax
import jax.numpy as jnp
from jax import lax
from jax.experimental import pallas as pl
from jax.experimental.pallas import tpu as pltpu

F32 = jnp.float32
BF16 = jnp.bfloat16

EPS = 1e-6
HEAD_DIM = 64
LANES = 128
TOPK = 16
N_KEYS = 128
CONV_PAD = 32
VMEM_LIMIT = 56 * 1024 * 1024
NEG_INF = float("-inf")
POS_INF = float("inf")
INV_SQRT2 = 1.0 / math.sqrt(2.0)
LOG2E = 1.0 / math.log(2.0)


def _nt(a, b):
    return lax.dot_general(a, b, (((1,), (1,)), ((), ())), preferred_element_type=F32)


def _mm(a, b):
    return jnp.dot(a, b, preferred_element_type=F32)


def _group_mean_sq(x, bd):
    sq = x * x
    hi = sq.astype(BF16)
    lo = (sq - hi.astype(F32)).astype(BF16)
    return _mm(hi, bd) + _mm(lo, bd)


def _params(*sem):
    return pltpu.CompilerParams(dimension_semantics=sem, vmem_limit_bytes=VMEM_LIMIT)


def _inproj_kernel(x_ref, g_ref, w_ref, bd_ref, gq_ref,
                   q_ref, k_ref, v_ref, kb_ref, vb_ref, glu_ref, qm_ref, *, sbw, cch):
    x = x_ref[...]
    xn = x * lax.rsqrt(jnp.mean(x * x, axis=-1, keepdims=True) + EPS) * g_ref[...]
    z = _mm(xn.astype(BF16), w_ref[...])
    q_ref[...] = (z[:, :sbw] * (LOG2E * HEAD_DIM ** -0.5)).astype(BF16)
    k = z[:, sbw:2 * sbw]
    v = z[:, 2 * sbw:3 * sbw]
    k_ref[...] = k
    v_ref[...] = v
    kb_ref[...] = k.astype(BF16)
    vb_ref[...] = v.astype(BF16)
    o = 3 * sbw
    glu_ref[...] = z[:, o:o + cch] * jax.nn.sigmoid(z[:, o + cch:o + 2 * cch])
    qm = z[:, o + 2 * cch:]
    qm_ref[...] = (qm * lax.rsqrt(_group_mean_sq(qm, bd_ref[...]) + EPS) * gq_ref[...]).astype(BF16)


def _inproj(x, g, w_bf, bd_mem, gq, *, sbw, cch, memw, tm):
    t, d = x.shape
    cols = w_bf.shape[1]
    row = lambda width: pl.BlockSpec((tm, width), lambda i: (i, 0))
    full = lambda a: pl.BlockSpec(a.shape, lambda i: (0,) * a.ndim)
    return pl.pallas_call(
        functools.partial(_inproj_kernel, sbw=sbw, cch=cch),
        grid=(t // tm,),
        in_specs=[row(d), full(g), full(w_bf), full(bd_mem), full(gq)],
        out_specs=[row(sbw), row(sbw), row(sbw), row(sbw), row(sbw), row(cch), row(memw)],
        out_shape=[jax.ShapeDtypeStruct((t, sbw), BF16), jax.ShapeDtypeStruct((t, sbw), F32),
                   jax.ShapeDtypeStruct((t, sbw), F32), jax.ShapeDtypeStruct((t, sbw), BF16),
                   jax.ShapeDtypeStruct((t, sbw), BF16), jax.ShapeDtypeStruct((t, cch), F32),
                   jax.ShapeDtypeStruct((t, memw), BF16)],
        compiler_params=_params("parallel"),
        name="inproj",
    )(x, g, w_bf, bd_mem, gq)


def _sb_weights(q, bias, kblk, tri, car, mask):
    ks, sub = kblk.shape[0], tri.shape[0]
    z = _nt(q, kblk) + bias
    neg_abs = lax.bitcast_convert_type(lax.bitcast_convert_type(z, jnp.uint32) | jnp.uint32(0x80000000), F32)
    lb = jnp.minimum(z, 0.0) - jnp.log2(1.0 + jnp.exp2(neg_abs))
    l1m = lb - z
    if mask is not None:
        l1m = jnp.where(mask, l1m, 0.0)
    ws = [None] * (ks // sub)
    for j in reversed(range(ks // sub)):
        cols = slice(j * sub, (j + 1) * sub)
        lj = l1m[:, cols].astype(BF16)
        tail = _mm(lj, tri)
        w = jnp.exp2(lb[:, cols] + tail + car)
        if mask is not None:
            w = jnp.where(mask[:, cols], w, 0.0)
        ws[j] = w.astype(BF16)
        total = tail[:, 0:1] + lj[:, 0:1].astype(F32)
        car = car + jnp.broadcast_to(total, car.shape)
    return (ws[0] if len(ws) == 1 else jnp.concatenate(ws, axis=1)), car


def _sb_prompt_kernel(bias_ref, q_ref, k_ref, v_ref, tri_ref, g_ref, o_ref, acc, car_a, car_b, *, bq):
    hp = pl.program_id(1)
    qi = pl.program_id(2)
    q2 = q_ref[...]
    first = lax.broadcasted_iota(jnp.int32, (bq, LANES), 1) < HEAD_DIM
    zero = jnp.zeros_like(q2)
    qa = jnp.where(first, q2, zero)
    qb = jnp.where(first, zero, q2)
    bias_a = bias_ref[2 * hp]
    bias_b = bias_ref[2 * hp + 1]
    tri = tri_ref[...]
    for r in (acc, car_a, car_b):
        r[...] = jnp.zeros_like(r)

    def step(kb, mask):
        off = pl.multiple_of(kb * bq, bq)
        kblk = k_ref[pl.ds(off, bq), :]
        vblk = v_ref[pl.ds(off, bq), :]
        vzero = jnp.zeros_like(vblk)
        wa, car_a[...] = _sb_weights(qa, bias_a, kblk, tri, car_a[...], mask)
        wb, car_b[...] = _sb_weights(qb, bias_b, kblk, tri, car_b[...], mask)
        acc[...] += _mm(wa, jnp.where(first, vblk, vzero)) + _mm(wb, jnp.where(first, vzero, vblk))

    rows = lax.broadcasted_iota(jnp.int32, (bq, bq), 0)
    cols = lax.broadcasted_iota(jnp.int32, (bq, bq), 1)
    step(qi, cols < rows)

    def body(i, c):
        step(qi - 1 - i, None)
        return c

    lax.fori_loop(0, qi, body, 0)

    o = acc[...]
    sq = o * o
    ms_a = jnp.sum(jnp.where(first, sq, 0.0), axis=-1, keepdims=True)
    ms_b = jnp.sum(jnp.where(first, 0.0, sq), axis=-1, keepdims=True)
    ms = jnp.where(first, ms_a, ms_b) * (1.0 / HEAD_DIM)
    o_ref[...] = (o * lax.rsqrt(ms + EPS) * g_ref[...]).astype(BF16)


def _tri_matrix(n):
    return (jnp.arange(n)[:, None] > jnp.arange(n)[None, :]).astype(BF16)


def _sb_prompt(q_bf, k_bf, v_bf, bias, g_row, *, batch, seq, bq, sub):
    t, sbw = q_bf.shape
    npair = sbw // LANES
    nq = seq // bq
    return pl.pallas_call(
        functools.partial(_sb_prompt_kernel, bq=bq),
        grid=(batch, npair, nq),
        in_specs=[pl.BlockSpec(memory_space=pltpu.SMEM),
                  pl.BlockSpec((bq, LANES), lambda b, p, i: (b * nq + i, p)),
                  pl.BlockSpec((seq, LANES), lambda b, p, i: (b, p)),
                  pl.BlockSpec((seq, LANES), lambda b, p, i: (b, p)),
                  pl.BlockSpec((sub, sub), lambda b, p, i: (0, 0)),
                  pl.BlockSpec((1, LANES), lambda b, p, i: (0, p))],
        out_specs=pl.BlockSpec((bq, LANES), lambda b, p, i: (b * nq + i, p)),
        out_shape=jax.ShapeDtypeStruct((t, sbw), BF16),
        scratch_shapes=[pltpu.VMEM((bq, LANES), F32), pltpu.VMEM((bq, sub), F32), pltpu.VMEM((bq, sub), F32)],
        compiler_params=_params("parallel", "parallel", "arbitrary"),
        name="sb_prompt",
    )(bias, q_bf, k_bf, v_bf, _tri_matrix(sub), g_row)


def _sb_sample_kernel(pt_ref, q_ref, kn_ref, vn_ref, bias_ref, tri_ref, bd_ref, g_ref, *rest,
                      pages, n_heads, n_new):
    k_refs = rest[:pages]
    v_refs = rest[pages:2 * pages]
    o_ref = rest[2 * pages]
    acc, car = rest[2 * pages + 1:]
    s = pl.program_id(1)
    q = q_ref[0]
    bias = bias_ref[...]
    tri = tri_ref[...]
    bk = tri.shape[0]
    nrow, width = q.shape

    @pl.when(s == 0)
    def _():
        pad = jnp.zeros((bk - kn_ref.shape[1], width), BF16)
        kblk = jnp.concatenate([kn_ref[0], pad], axis=0)
        vblk = jnp.concatenate([vn_ref[0], pad], axis=0)
        rows = lax.broadcasted_iota(jnp.int32, (nrow, bk), 0)
        cols = lax.broadcasted_iota(jnp.int32, (nrow, bk), 1)
        mask = cols < rows // n_heads
        w, car[...] = _sb_weights(q, bias, kblk, tri, jnp.zeros(car.shape, F32), mask)
        acc[...] = _mm(w, vblk)

    for r in range(pages):
        w, car[...] = _sb_weights(q, bias, k_refs[r][0].astype(BF16), tri, car[...], None)
        acc[...] += _mm(w, v_refs[r][0].astype(BF16))

    @pl.when(s == pl.num_programs(1) - 1)
    def _():
        rows = lax.broadcasted_iota(jnp.int32, (nrow, width), 0)
        cols = lax.broadcasted_iota(jnp.int32, (nrow, width), 1)
        own = jnp.where(cols // HEAD_DIM == rows % n_heads, acc[...], 0.0)
        o = jnp.sum(own.reshape(n_new, n_heads, width), axis=1)
        o = o * lax.rsqrt(_group_mean_sq(o, bd_ref[...]) + EPS) * g_ref[...]
        o_ref[0] = o.astype(BF16)


def _sb_sample(q_bf, kn_bf, vn_bf, bias, g_row, bd_sb, pool_k, pool_v, page_table, *, n_new, pages):
    nb, npg = page_table.shape
    tok, width = pool_k.shape[1:]
    n_heads = width // HEAD_DIM
    nrow = n_new * n_heads
    nstep = npg // pages
    q3 = q_bf.reshape(nb, n_new, width)
    head_of_lane = jnp.arange(width) // HEAD_DIM
    qbd = jnp.where(head_of_lane[None, None, None, :] == jnp.arange(n_heads)[None, None, :, None],
                    q3[:, :, None, :], jnp.zeros((), BF16)).reshape(nb, nrow, width)
    pad_new = 8 - n_new
    kn = jnp.pad(kn_bf.reshape(nb, n_new, width), ((0, 0), (0, pad_new), (0, 0)))
    vn = jnp.pad(vn_bf.reshape(nb, n_new, width), ((0, 0), (0, pad_new), (0, 0)))
    bias_rows = jnp.broadcast_to(jnp.tile(bias, n_new)[:, None], (nrow, LANES)).astype(F32)

    def page_spec(r):
        return pl.BlockSpec((1, tok, width), lambda b, s, pt: (pt[b, npg - 1 - (s * pages + r)], 0, 0))

    const = lambda shape: pl.BlockSpec(shape, lambda b, s, pt: (0,) * len(shape))
    per_b = lambda shape: pl.BlockSpec(shape, lambda b, s, pt: (b,) + (0,) * (len(shape) - 1))
    grid_spec = pltpu.PrefetchScalarGridSpec(
        num_scalar_prefetch=1,
        grid=(nb, nstep),
        in_specs=[per_b((1, nrow, width)), per_b((1, 8, width)), per_b((1, 8, width)),
                  const((nrow, LANES)), const((tok, tok)), const((width, width)), const((1, width))]
                 + [page_spec(r) for r in range(pages)] * 2,
        out_specs=per_b((1, n_new, width)),
        scratch_shapes=[pltpu.VMEM((nrow, width), F32), pltpu.VMEM((nrow, tok), F32)],
    )
    return pl.pallas_call(
        functools.partial(_sb_sample_kernel, pages=pages, n_heads=n_heads, n_new=n_new),
        grid_spec=grid_spec,
        out_shape=jax.ShapeDtypeStruct((nb, n_new, width), BF16),
        compiler_params=_params("parallel", "arbitrary"),
        name="sb_sample",
    )(page_table, qbd, kn, vn, bias_rows, _tri_matrix(tok), bd_sb, g_row,
      *([pool_k] * pages), *([pool_v] * pages))


def _conv_kernel(*refs, ts, taps, chunk, has_prev):
    if has_prev:
        hist_ref, prev_ref, cur_ref, w_ref, b_ref, lg_ref, lb_ref, o_ref, buf = refs
        head = jnp.where(pl.program_id(1) == 0, hist_ref[0], prev_ref[0])
    else:
        hist_ref, cur_ref, w_ref, b_ref, lg_ref, lb_ref, o_ref, buf = refs
        head = hist_ref[0]
    buf[0:CONV_PAD, :] = head
    buf[CONV_PAD:CONV_PAD + ts, :] = cur_ref[0]
    first = CONV_PAD - (taps - 1)
    for c0 in range(0, ts, chunk):
        acc = jnp.zeros((chunk, buf.shape[1]), F32)
        for k in range(taps):
            acc = acc + w_ref[k:k + 1, :] * buf[first + c0 + k:first + c0 + k + chunk, :]
        c = acc + b_ref[...]
        xc = c - jnp.mean(c, axis=-1, keepdims=True)
        y = xc * lax.rsqrt(jnp.mean(xc * xc, axis=-1, keepdims=True) + EPS) * lg_ref[...] + lb_ref[...]
        o_ref[0, c0:c0 + chunk, :] = (y * jax.nn.sigmoid(y)).astype(BF16)


def _conv_module(glu3, hist_pad, w, b, lg, lb, *, ts):
    nb, s, cch = glu3.shape
    taps = w.shape[0]
    has_prev = s > ts
    chunk = min(ts, 64)
    cur = pl.BlockSpec((1, ts, cch), lambda bb, i: (bb, i, 0))
    hist = pl.BlockSpec((1, CONV_PAD, cch), lambda bb, i: (bb, 0, 0))
    const = lambda a: pl.BlockSpec(a.shape, lambda bb, i: (0,) * a.ndim)
    in_specs, args = [hist], [hist_pad]
    if has_prev:
        per = ts // CONV_PAD
        in_specs.append(pl.BlockSpec((1, CONV_PAD, cch), lambda bb, i: (bb, jnp.maximum(i * per - 1, 0), 0)))
        args.append(glu3)
    in_specs += [cur, const(w), const(b), const(lg), const(lb)]
    args += [glu3, w, b, lg, lb]
    return pl.pallas_call(
        functools.partial(_conv_kernel, ts=ts, taps=taps, chunk=chunk, has_prev=has_prev),
        grid=(nb, s // ts),
        in_specs=in_specs,
        out_specs=cur,
        out_shape=jax.ShapeDtypeStruct((nb, s, cch), BF16),
        scratch_shapes=[pltpu.VMEM((CONV_PAD + max(ts, 8), cch), F32)],
        compiler_params=_params("parallel", "parallel"),
        name="conv_module",
    )(*args)


def _memkv_kernel(m_ref, g_ref, wk_ref, wv_ref, bd_ref, gk_ref, k_ref, v_ref):
    x = m_ref[...]
    xn = (x * lax.rsqrt(jnp.mean(x * x, axis=-1, keepdims=True) + EPS) * g_ref[...]).astype(BF16)
    k = _mm(xn, wk_ref[...])
    k_ref[...] = k * lax.rsqrt(_group_mean_sq(k, bd_ref[...]) + EPS) * gk_ref[...]
    v_ref[...] = _mm(xn, wv_ref[...])


def _memkv(mem, g, wk_bf, wv_bf, bd_mem, gk, *, tm):
    t, d = mem.shape
    memw = wk_bf.shape[1]
    full = lambda a: pl.BlockSpec(a.shape, lambda i: (0,) * a.ndim)
    out = pl.BlockSpec((tm, memw), lambda i: (i, 0))
    return pl.pallas_call(
        _memkv_kernel,
        grid=(t // tm,),
        in_specs=[pl.BlockSpec((tm, d), lambda i: (i, 0)), full(g), full(wk_bf), full(wv_bf), full(bd_mem), full(gk)],
        out_specs=[out, out],
        out_shape=[jax.ShapeDtypeStruct((t, memw), F32)] * 2,
        compiler_params=_params("parallel"),
        name="memkv",
    )(mem, g, wk_bf, wv_bf, bd_mem, gk)


def _mem_attn_kernel(q_ref, k_ref, v_ref, bd_ref, g_ref, o_ref, *, n_heads):
    q = q_ref[0]
    tq = q.shape[0]
    rows = max(tq, 8)
    if rows != tq:
        q = jnp.concatenate([q, jnp.zeros((rows - tq, q.shape[1]), q.dtype)], axis=0)
    kb = k_ref[0].astype(BF16)
    vb = v_ref[0].astype(BF16)
    head = lax.broadcasted_iota(jnp.int32, q.shape, 1) // HEAD_DIM
    zero = jnp.zeros_like(q)
    out = jnp.zeros(q.shape, F32)
    for h in range(n_heads):
        s = _nt(jnp.where(head == h, q, zero), kb) * (HEAD_DIM ** -0.5)
        p = jnp.exp(s - jnp.max(s, axis=-1, keepdims=True))
        oh = _mm(p.astype(BF16), vb) / jnp.sum(p, axis=-1, keepdims=True)
        out = jnp.where(head == h, oh, out)
    out = out * lax.rsqrt(_group_mean_sq(out, bd_ref[...]) + EPS) * g_ref[...]
    o_ref[0] = out[:tq].astype(BF16)


def _mem_attn(q3, k3, v3, bd_mem, g_row, *, tq):
    ng, s, memw = q3.shape
    m = k3.shape[1]
    const = lambda a: pl.BlockSpec(a.shape, lambda gq, i: (0,) * a.ndim)
    qspec = pl.BlockSpec((1, tq, memw), lambda gq, i: (gq, i, 0))
    kspec = pl.BlockSpec((1, m, memw), lambda gq, i: (gq, 0, 0))
    return pl.pallas_call(
        functools.partial(_mem_attn_kernel, n_heads=memw // HEAD_DIM),
        grid=(ng, s // tq),
        in_specs=[qspec, kspec, kspec, const(bd_mem), const(g_row)],
        out_specs=qspec,
        out_shape=jax.ShapeDtypeStruct((ng, s, memw), BF16),
        compiler_params=_params("parallel", "parallel"),
        name="mem_attn",
    )(q3, k3, v3, bd_mem, g_row)


def _outproj_kernel(x_ref, sb_ref, cv_ref, mm_ref, w_ref, o_ref):
    a = sb_ref.shape[1]
    b = a + cv_ref.shape[1]
    y = _mm(sb_ref[...], w_ref[0:a, :]) + _mm(cv_ref[...], w_ref[a:b, :]) + _mm(mm_ref[...], w_ref[b:, :])
    o_ref[...] = x_ref[...] + y


def _outproj(x, o_sb, o_conv, o_mem, w_bf, *, tm):
    t, d = x.shape
    row = lambda a: pl.BlockSpec((tm, a.shape[1]), lambda i: (i, 0))
    return pl.pallas_call(
        _outproj_kernel,
        grid=(t // tm,),
        in_specs=[row(x), row(o_sb), row(o_conv), row(o_mem), pl.BlockSpec(w_bf.shape, lambda i: (0, 0))],
        out_specs=row(x),
        out_shape=jax.ShapeDtypeStruct((t, d), F32),
        compiler_params=_params("parallel"),
        name="outproj",
    )(x, o_sb, o_conv, o_mem, w_bf)


def _top_values(s, n, with_rank=False):
    tops = []
    cur = s
    rank = jnp.full(s.shape, float(n), F32)
    for k in range(n):
        m = jnp.max(cur, axis=0, keepdims=True)
        tops.append(m)
        hit = cur >= jnp.where(m == NEG_INF, POS_INF, m)
        if with_rank:
            rank = jnp.where(hit, float(k), rank)
        cur = jnp.where(hit, NEG_INF, cur)
    return (tops, rank) if with_rank else tops


def _peer_select_tile(s1, s2):
    t1 = jnp.concatenate(_top_values(s1, TOPK), axis=0)
    tops2, rank2 = _top_values(s2, TOPK, with_rank=True)
    t2 = jnp.concatenate(tops2, axis=0)
    row8 = lax.broadcasted_iota(jnp.int32, (8, LANES), 0)
    parts = [t1[0:1] + t2]
    for a in range(1, 8):
        parts.append(jnp.where(row8 < TOPK // (a + 1), t1[a:a + 1] + t2[0:8], NEG_INF))
    parts.append(t1[8:TOPK] + t2[0:1])
    best = _top_values(jnp.concatenate(parts, axis=0), TOPK)
    tau = best[TOPK - 1]
    top = best[0]
    zsum = jnp.zeros_like(top)
    for bv in best:
        zsum = zsum + jnp.exp(bv - top)
    cnt = jnp.zeros(s1.shape, F32)
    for a in range(TOPK):
        sel = (t1[a:a + 1] + t2) >= tau
        cnt_a = jnp.sum(jnp.where(sel, 1.0, 0.0), axis=0, keepdims=True)
        cnt = jnp.where(s1 == t1[a:a + 1], cnt_a, cnt)
    e1 = jnp.exp(s1 - t1[0:1]) / zsum
    e2 = jnp.exp(s2 - t2[0:1])
    return cnt, e1, rank2, e2


def _peer_select_kernel(x_ref, g_ref, wq_ref, keys_ref, xn_ref, cnt_ref, e1_ref, rk_ref, e2_ref,
                        qt_scr, s1_scr, s2_scr, *, n_heads):
    x = x_ref[...]
    xn = (x * lax.rsqrt(jnp.mean(x * x, axis=-1, keepdims=True) + EPS) * g_ref[...]).astype(BF16)
    xn_ref[...] = xn
    qt_scr[...] = _nt(wq_ref[...], xn).astype(BF16)
    dh = keys_ref.shape[2]
    groups = x.shape[0] // LANES

    def head(h, c):
        r1 = pl.multiple_of(h * 2 * dh, 2 * dh)
        r2 = pl.multiple_of(h * 2 * dh + dh, dh)
        s1_scr[...] = _mm(keys_ref[2 * h], qt_scr[pl.ds(r1, dh), :])
        s2_scr[...] = _mm(keys_ref[2 * h + 1], qt_scr[pl.ds(r2, dh), :])
        for gi in range(groups):
            lanes = slice(gi * LANES, (gi + 1) * LANES)
            cnt, e1, rank2, e2 = _peer_select_tile(s1_scr[:, lanes], s2_scr[:, lanes])
            cnt_ref[h, gi] = cnt
            e1_ref[h, gi] = e1
            rk_ref[h, gi] = pltpu.bitcast(rank2.astype(BF16), jnp.uint32)
            e2_ref[h, gi] = pltpu.bitcast(e2.astype(BF16), jnp.uint32)
        return c

    lax.fori_loop(0, n_heads, head, 0)


def _peer_select(x, g, wq_t_bf, keys_bf, *, tm):
    t, d = x.shape
    n_heads = keys_bf.shape[0] // 2
    groups = tm // LANES
    full = lambda a: pl.BlockSpec(a.shape, lambda i: (0,) * a.ndim)
    sel = pl.BlockSpec((n_heads, groups, N_KEYS, LANES), lambda i: (0, i, 0, 0))
    row_shape = jax.ShapeDtypeStruct((n_heads, t // LANES, N_KEYS, LANES), F32)
    col = pl.BlockSpec((n_heads, groups, N_KEYS // 2, LANES), lambda i: (0, i, 0, 0))
    col_shape = jax.ShapeDtypeStruct((n_heads, t // LANES, N_KEYS // 2, LANES), jnp.uint32)
    return pl.pallas_call(
        functools.partial(_peer_select_kernel, n_heads=n_heads),
        grid=(t // tm,),
        in_specs=[pl.BlockSpec((tm, d), lambda i: (i, 0)), full(g), full(wq_t_bf), full(keys_bf)],
        out_specs=[pl.BlockSpec((tm, d), lambda i: (i, 0)), sel, sel, col, col],
        out_shape=[jax.ShapeDtypeStruct((t, d), BF16), row_shape, row_shape, col_shape, col_shape],
        scratch_shapes=[pltpu.VMEM((wq_t_bf.shape[0], tm), BF16), pltpu.VMEM((N_KEYS, tm), F32),
                        pltpu.VMEM((N_KEYS, tm), F32)],
        compiler_params=_params("parallel"),
        name="peer_select",
    )(x, g, wq_t_bf, keys_bf)


def _peer_dense_kernel(xn_ref, u_ref, vt_ref, cnt_ref, e1_ref, rk_ref, e2_ref, xmid_ref, o_ref,
                       a_scr, wa_scr, acc_scr, *, n_blocks, splits):
    g = pl.program_id(0)
    n_heads, groups = cnt_ref.shape[:2]
    eb = u_ref.shape[0]
    rows_i = eb // N_KEYS
    done = jnp.maximum(g - 1, 0)
    blk = done % n_blocks

    @pl.when(g == 0)
    def _():
        a_scr[...] = jnp.zeros_like(a_scr)

    xn = xn_ref[...]
    contrib = None
    for s in range(splits):
        for r in range(s * rows_i // splits, (s + 1) * rows_i // splits):
            i_row = blk * rows_i + r
            rows = slice(r * N_KEYS, (r + 1) * N_KEYS)
            for gi in range(groups):
                lanes = slice(gi * LANES, (gi + 1) * LANES)
                a = a_scr[rows, lanes]
                act = (0.5 * a * (1.0 + lax.erf(a * INV_SQRT2))).astype(BF16)
                gate = jnp.zeros((N_KEYS, LANES), BF16)
                for h in range(n_heads):
                    cnt = jnp.broadcast_to(cnt_ref[h, gi, pl.ds(i_row, 1), :], (N_KEYS, LANES)).astype(BF16)
                    e1 = jnp.broadcast_to(e1_ref[h, gi, pl.ds(i_row, 1), :], (N_KEYS, LANES)).astype(BF16)
                    rk = pltpu.bitcast(rk_ref[h, gi], BF16)
                    e2 = pltpu.bitcast(e2_ref[h, gi], BF16)
                    gate = gate + jnp.where(rk < cnt, e2, jnp.zeros_like(e2)) * e1
                wa_scr[rows, lanes] = gate * act
        part = slice(s * eb // splits, (s + 1) * eb // splits)
        a_scr[part, :] = _nt(u_ref[part, :], xn)
        c = _mm(vt_ref[:, part], wa_scr[part, :])
        contrib = c if contrib is None else contrib + c

    @pl.when(blk == 0)
    def _():
        acc_scr[...] = contrib

    @pl.when(blk != 0)
    def _():
        acc_scr[...] += contrib

    @pl.when((blk == n_blocks - 1) & (g > 0))
    def _():
        o_ref[...] = xmid_ref[...] + acc_scr[...].T


def _peer_dense(xn_bf, u_bf, vt_bf, cnt, e1, rk, e2, x_mid, *, tm, eb):
    t, d = x_mid.shape
    n_blocks = u_bf.shape[0] // eb
    n_items = (t // tm) * n_blocks
    n_heads = cnt.shape[0]
    groups = tm // LANES
    nxt = lambda g: jnp.minimum(g, n_items - 1)
    cur = lambda g: jnp.maximum(g - 1, 0)
    tok = pl.BlockSpec((tm, d), lambda g: (cur(g) // n_blocks, 0))
    sel = pl.BlockSpec((n_heads, groups, N_KEYS, LANES), lambda g: (0, cur(g) // n_blocks, 0, 0))
    col = pl.BlockSpec((n_heads, groups, N_KEYS // 2, LANES), lambda g: (0, cur(g) // n_blocks, 0, 0))
    return pl.pallas_call(
        functools.partial(_peer_dense_kernel, n_blocks=n_blocks, splits=2),
        grid=(n_items + 1,),
        in_specs=[pl.BlockSpec((tm, d), lambda g: (nxt(g) // n_blocks, 0)),
                  pl.BlockSpec((eb, d), lambda g: (nxt(g) % n_blocks, 0)),
                  pl.BlockSpec((d, eb), lambda g: (0, cur(g) % n_blocks)),
                  sel, sel, col, col, tok],
        out_specs=tok,
        out_shape=jax.ShapeDtypeStruct((t, d), F32),
        scratch_shapes=[pltpu.VMEM((eb, tm), F32), pltpu.VMEM((eb, tm), BF16), pltpu.VMEM((d, tm), F32)],
        compiler_params=_params("arbitrary"),
        name="peer_dense",
    )(xn_bf, u_bf, vt_bf, cnt, e1, rk, e2, x_mid)


def _block_diag(width):
    idx = jnp.arange(width) // HEAD_DIM
    return jnp.where(idx[:, None] == idx[None, :], 1.0 / HEAD_DIM, 0.0).astype(BF16)


def _tile(n, want):
    return want if n % want == 0 else n


def kernel(x_prompt, x_sample, mem_prompt, cache_sb_k, cache_sb_v, page_table, cache_mem_k, cache_mem_v,
           state_conv, g_mix, w_in, sb_bias, sb_out_g, conv_w, conv_b, conv_ln_g, conv_ln_b, g_mem, w_mem_k,
           w_mem_v, mem_q_g, mem_k_g, mem_out_g, w_out, g_ffn, peer_w_q, peer_sub_keys, peer_u, peer_v):
    depth = w_in.shape[0]
    batch, seq, d = x_prompt.shape
    nb, n_new, _ = x_sample.shape
    sb_heads = sb_bias.shape[1]
    sbw = sb_heads * HEAD_DIM
    cch = conv_w.shape[2]
    taps = conv_w.shape[1]
    mem_heads = mem_out_g.shape[1]
    memw = mem_heads * HEAD_DIM
    mem_tok = mem_prompt.shape[1]
    n_pool, page = cache_sb_k.shape[1:3]
    n_exp = peer_u.shape[1]

    bd_mem = _block_diag(memw)
    bd_sb = _block_diag(sbw)
    xp = x_prompt.reshape(batch * seq, d)
    xs = x_sample.reshape(nb * n_new, d)
    tp = _tile(batch * seq, 512)
    tsm = _tile(nb * n_new, 512)
    outs = [[] for _ in range(8)]

    for l in range(depth):
        row = lambda a: a[l].reshape(1, -1)
        w_in_bf = w_in[l].astype(BF16)
        bias2 = sb_bias[l] * LOG2E
        w_out_bf = w_out[l].astype(BF16)
        gq = jnp.tile(mem_q_g[l], mem_heads).reshape(1, memw)
        gk = jnp.tile(mem_k_g[l], mem_heads).reshape(1, memw)
        conv_args = (conv_w[l], row(conv_b), row(conv_ln_g), row(conv_ln_b))

        mk, mv = _memkv(mem_prompt.reshape(batch * mem_tok, d), row(g_mem), w_mem_k[l].astype(BF16),
                        w_mem_v[l].astype(BF16), bd_mem, gk, tm=_tile(batch * mem_tok, 512))

        q, k, v, kb, vb, glu, qm = _inproj(xp, row(g_mix), w_in_bf, bd_mem, gq, sbw=sbw, cch=cch, memw=memw, tm=tp)
        o_sb = _sb_prompt(q, kb, vb, bias2, row(sb_out_g), batch=batch, seq=seq,
                          bq=_tile(seq, 512), sub=256)
        glu3 = glu.reshape(batch, seq, cch)
        o_conv = _conv_module(glu3, jnp.zeros((batch, CONV_PAD, cch), F32), *conv_args, ts=_tile(seq, 256))
        o_mem = _mem_attn(qm.reshape(batch, seq, memw), mk.reshape(batch, mem_tok, memw),
                          mv.reshape(batch, mem_tok, memw), bd_mem, row(mem_out_g), tq=_tile(seq, 512))
        xp_mid = _outproj(xp, o_sb, o_conv.reshape(-1, cch), o_mem.reshape(-1, memw), w_out_bf, tm=tp)

        q_s, k_s, v_s, kb_s, vb_s, glu_s, qm_s = _inproj(xs, row(g_mix), w_in_bf, bd_mem, gq,
                                                        sbw=sbw, cch=cch, memw=memw, tm=tsm)
        npg = page_table.shape[1]
        o_sb_s = _sb_sample(q_s, kb_s, vb_s, bias2, row(sb_out_g), bd_sb,
                            cache_sb_k[l].reshape(n_pool, page, sbw), cache_sb_v[l].reshape(n_pool, page, sbw),
                            page_table, n_new=n_new, pages=8 if npg % 8 == 0 else npg)
        glu_s3 = glu_s.reshape(nb, n_new, cch)
        hist_s = jnp.pad(state_conv[l], ((0, 0), (CONV_PAD - (taps - 1), 0), (0, 0)))
        o_conv_s = _conv_module(glu_s3, hist_s, *conv_args, ts=n_new)
        o_mem_s = _mem_attn(qm_s.reshape(nb, n_new, memw), cache_mem_k[l].reshape(nb, mem_tok, memw),
                            cache_mem_v[l].reshape(nb, mem_tok, memw), bd_mem, row(mem_out_g), tq=n_new)
        xs_mid = _outproj(xs, o_sb_s.reshape(-1, sbw), o_conv_s.reshape(-1, cch), o_mem_s.reshape(-1, memw),
                          w_out_bf, tm=tsm)

        wq_t = peer_w_q[l].T.astype(BF16)
        keys = peer_sub_keys[l].reshape(-1, N_KEYS, peer_sub_keys.shape[-1]).astype(BF16)
        u_bf = peer_u[l].astype(BF16)
        vt_bf = peer_v[l].T.astype(BF16)
        new = []
        for x_mid, tm in ((xp_mid, tp), (xs_mid, tsm)):
            xn, th, e1, s2, e2 = _peer_select(x_mid, row(g_ffn), wq_t, keys, tm=tm)
            new.append(_peer_dense(xn, u_bf, vt_bf, th, e1, s2, e2, x_mid, tm=tm, eb=_tile(n_exp, 1024)))
        xp, xs = new

        hist_rows = taps - 1
        cs = jnp.concatenate([state_conv[l], glu_s3], axis=1)[:, -hist_rows:]
        for lst, val in zip(outs, (k.reshape(batch, seq, sb_heads, HEAD_DIM), v.reshape(batch, seq, sb_heads, HEAD_DIM),
                                   k_s.reshape(nb, n_new, sb_heads, HEAD_DIM), v_s.reshape(nb, n_new, sb_heads, HEAD_DIM),
                                   mk.reshape(batch, mem_tok, mem_heads, HEAD_DIM),
                                   mv.reshape(batch, mem_tok, mem_heads, HEAD_DIM),
                                   glu3[:, seq - hist_rows:], cs)):
            lst.append(val)

    return (xp.reshape(batch, seq, d), xs.reshape(nb, n_new, d), *[jnp.stack(o) for o in outs])
```

```python
import functools
import math

import jax
import jax.numpy as jnp
from jax import lax
from jax.experimental import pallas as pl
from jax.experimental.pallas import tpu as pltpu

F32 = jnp.float32
BF16 = jnp.bfloat16

EPS = 1e-6
HEAD_DIM = 64
LANES = 128
TOPK = 16
N_KEYS = 128
CONV_PAD = 32
VMEM_LIMIT = 56 * 1024 * 1024
NEG_INF = float("-inf")
POS_INF = float("inf")
INV_SQRT2 = 1.0 / math.sqrt(2.0)
LOG2E = 1.0 / math.log(2.0)


def _nt(a, b):
    return lax.dot_general(a, b, (((1,), (1,)), ((), ())), preferred_element_type=F32)


def _mm(a, b):
    return jnp.dot(a, b, preferred_element_type=F32)


def _group_mean_sq(x, bd):
    sq = x * x
    hi = sq.astype(BF16)
    lo = (sq - hi.astype(F32)).astype(BF16)
    return _mm(hi, bd) + _mm(lo, bd)


def _params(*sem):
    return pltpu.CompilerParams(dimension_semantics=sem, vmem_limit_bytes=VMEM_LIMIT)


def _inproj_kernel(x_ref, g_ref, w_ref, bd_ref, gq_ref,
                   q_ref, k_ref, v_ref, kb_ref, vb_ref, glu_ref, qm_ref, *, sbw, cch):
    x = x_ref[...]
    xn = x * lax.rsqrt(jnp.mean(x * x, axis=-1, keepdims=True) + EPS) * g_ref[...]
    z = _mm(xn.astype(BF16), w_ref[...])
    q_ref[...] = (z[:, :sbw] * (LOG2E * HEAD_DIM ** -0.5)).astype(BF16)
    k = z[:, sbw:2 * sbw]
    v = z[:, 2 * sbw:3 * sbw]
    k_ref[...] = k
    v_ref[...] = v
    kb_ref[...] = k.astype(BF16)
    vb_ref[...] = v.astype(BF16)
    o = 3 * sbw
    glu_ref[...] = z[:, o:o + cch] * jax.nn.sigmoid(z[:, o + cch:o + 2 * cch])
    qm = z[:, o + 2 * cch:]
    qm_ref[...] = (qm * lax.rsqrt(_group_mean_sq(qm, bd_ref[...]) + EPS) * gq_ref[...]).astype(BF16)


def _inproj(x, g, w_bf, bd_mem, gq, *, sbw, cch, memw, tm):
    t, d = x.shape
    cols = w_bf.shape[1]
    row = lambda width: pl.BlockSpec((tm, width), lambda i: (i, 0))
    full = lambda a: pl.BlockSpec(a.shape, lambda i: (0,) * a.ndim)
    return pl.pallas_call(
        functools.partial(_inproj_kernel, sbw=sbw, cch=cch),
        grid=(t // tm,),
        in_specs=[row(d), full(g), full(w_bf), full(bd_mem), full(gq)],
        out_specs=[row(sbw), row(sbw), row(sbw), row(sbw), row(sbw), row(cch), row(memw)],
        out_shape=[jax.ShapeDtypeStruct((t, sbw), BF16), jax.ShapeDtypeStruct((t, sbw), F32),
                   jax.ShapeDtypeStruct((t, sbw), F32), jax.ShapeDtypeStruct((t, sbw), BF16),
                   jax.ShapeDtypeStruct((t, sbw), BF16), jax.ShapeDtypeStruct((t, cch), F32),
                   jax.ShapeDtypeStruct((t, memw), BF16)],
        compiler_params=_params("parallel"),
        name="inproj",
    )(x, g, w_bf, bd_mem, gq)


def _sb_weights(q, bias, kblk, tri, car, mask):
    ks, sub = kblk.shape[0], tri.shape[0]
    z = _nt(q, kblk) + bias
    neg_abs = lax.bitcast_convert_type(lax.bitcast_convert_type(z, jnp.uint32) | jnp.uint32(0x80000000), F32)
    lb = jnp.minimum(z, 0.0) - jnp.log2(1.0 + jnp.exp2(neg_abs))
    l1m = lb - z
    if mask is not None:
        l1m = jnp.where(mask, l1m, 0.0)
    ws = [None] * (ks // sub)
    for j in reversed(range(ks // sub)):
        cols = slice(j * sub, (j + 1) * sub)
        lj = l1m[:, cols].astype(BF16)
        tail = _mm(lj, tri)
        w = jnp.exp2(lb[:, cols] + tail + car)
        if mask is not None:
            w = jnp.where(mask[:, cols], w, 0.0)
        ws[j] = w.astype(BF16)
        total = tail[:, 0:1] + lj[:, 0:1].astype(F32)
        car = car + jnp.broadcast_to(total, car.shape)
    return (ws[0] if len(ws) == 1 else jnp.concatenate(ws, axis=1)), car


def _sb_prompt_kernel(bias_ref, q_ref, k_ref, v_ref, tri_ref, g_ref, o_ref, acc, car_a, car_b, *, bq):
    hp = pl.program_id(1)
    qi = pl.program_id(2)
    q2 = q_ref[...]
    first = lax.broadcasted_iota(jnp.int32, (bq, LANES), 1) < HEAD_DIM
    zero = jnp.zeros_like(q2)
    qa = jnp.where(first, q2, zero)
    qb = jnp.where(first, zero, q2)
    bias_a = bias_ref[2 * hp]
    bias_b = bias_ref[2 * hp + 1]
    tri = tri_ref[...]
    for r in (acc, car_a, car_b):
        r[...] = jnp.zeros_like(r)

    def step(kb, mask):
        off = pl.multiple_of(kb * bq, bq)
        kblk = k_ref[pl.ds(off, bq), :]
        vblk = v_ref[pl.ds(off, bq), :]
        vzero = jnp.zeros_like(vblk)
        wa, car_a[...] = _sb_weights(qa, bias_a, kblk, tri, car_a[...], mask)
        wb, car_b[...] = _sb_weights(qb, bias_b, kblk, tri, car_b[...], mask)
        acc[...] += _mm(wa, jnp.where(first, vblk, vzero)) + _mm(wb, jnp.where(first, vzero, vblk))

    rows = lax.broadcasted_iota(jnp.int32, (bq, bq), 0)
    cols = lax.broadcasted_iota(jnp.int32, (bq, bq), 1)
    step(qi, cols < rows)

    def body(i, c):
        step(qi - 1 - i, None)
        return c

    lax.fori_loop(0, qi, body, 0)

    o = acc[...]
    sq = o * o
    ms_a = jnp.sum(jnp.where(first, sq, 0.0), axis=-1, keepdims=True)
    ms_b = jnp.sum(jnp.where(first, 0.0, sq), axis=-1, keepdims=True)
    ms = jnp.where(first, ms_a, ms_b) * (1.0 / HEAD_DIM)
    o_ref[...] = (o * lax.rsqrt(ms + EPS) * g_ref[...]).astype(BF16)


def _tri_matrix(n):
    return (jnp.arange(n)[:, None] > jnp.arange(n)[None, :]).astype(BF16)


def _sb_prompt(q_bf, k_bf, v_bf, bias, g_row, *, batch, seq, bq, sub):
    t, sbw = q_bf.shape
    npair = sbw // LANES
    nq = seq // bq
    return pl.pallas_call(
        functools.partial(_sb_prompt_kernel, bq=bq),
        grid=(batch, npair, nq),
        in_specs=[pl.BlockSpec(memory_space=pltpu.SMEM),
                  pl.BlockSpec((bq, LANES), lambda b, p, i: (b * nq + i, p)),
                  pl.BlockSpec((seq, LANES), lambda b, p, i: (b, p)),
                  pl.BlockSpec((seq, LANES), lambda b, p, i: (b, p)),
                  pl.BlockSpec((sub, sub), lambda b, p, i: (0, 0)),
                  pl.BlockSpec((1, LANES), lambda b, p, i: (0, p))],
        out_specs=pl.BlockSpec((bq, LANES), lambda b, p, i: (b * nq + i, p)),
        out_shape=jax.ShapeDtypeStruct((t, sbw), BF16),
        scratch_shapes=[pltpu.VMEM((bq, LANES), F32), pltpu.VMEM((bq, sub), F32), pltpu.VMEM((bq, sub), F32)],
        compiler_params=_params("parallel", "parallel", "arbitrary"),
        name="sb_prompt",
    )(bias, q_bf, k_bf, v_bf, _tri_matrix(sub), g_row)


def _sb_sample_kernel(pt_ref, q_ref, kn_ref, vn_ref, bias_ref, tri_ref, bd_ref, g_ref, *rest,
                      pages, n_heads, n_new):
    k_refs = rest[:pages]
    v_refs = rest[pages:2 * pages]
    o_ref = rest[2 * pages]
    acc, car = rest[2 * pages + 1:]
    s = pl.program_id(1)
    q = q_ref[0]
    bias = bias_ref[...]
    tri = tri_ref[...]
    bk = tri.shape[0]
    nrow, width = q.shape

    @pl.when(s == 0)
    def _():
        pad = jnp.zeros((bk - kn_ref.shape[1], width), BF16)
        kblk = jnp.concatenate([kn_ref[0], pad], axis=0)
        vblk = jnp.concatenate([vn_ref[0], pad], axis=0)
        rows = lax.broadcasted_iota(jnp.int32, (nrow, bk), 0)
        cols = lax.broadcasted_iota(jnp.int32, (nrow, bk), 1)
        mask = cols < rows // n_heads
        w, car[...] = _sb_weights(q, bias[:, :bk], kblk, tri, jnp.zeros(car.shape, F32), mask)
        acc[...] = _mm(w, vblk)

    def page(ref):
        heads = [ref[0, pl.ds(h, bk, stride=n_heads), :] for h in range(n_heads)]
        return jnp.concatenate(heads, axis=1).astype(BF16)

    kblk = jnp.concatenate([page(k_refs[r]) for r in reversed(range(pages))], axis=0)
    vblk = jnp.concatenate([page(v_refs[r]) for r in reversed(range(pages))], axis=0)
    w, car[...] = _sb_weights(q, bias, kblk, tri, car[...], None)
    acc[...] += _mm(w, vblk)

    @pl.when(s == pl.num_programs(1) - 1)
    def _():
        rows = lax.broadcasted_iota(jnp.int32, (nrow, width), 0)
        cols = lax.broadcasted_iota(jnp.int32, (nrow, width), 1)
        own = jnp.where(cols // HEAD_DIM == rows % n_heads, acc[...], 0.0)
        o = jnp.sum(own.reshape(n_new, n_heads, width), axis=1)
        o = o * lax.rsqrt(_group_mean_sq(o, bd_ref[...]) + EPS) * g_ref[...]
        o_ref[0] = o.astype(BF16)


def _sb_sample(q_bf, kn_bf, vn_bf, bias, g_row, bd_sb, pool_k, pool_v, page_table, *, layer, n_new, pages):
    nb, npg = page_table.shape
    tok, n_heads = pool_k.shape[2:4]
    width = n_heads * HEAD_DIM
    nrow = n_new * n_heads
    nstep = npg // pages
    q3 = q_bf.reshape(nb, n_new, width)
    head_of_lane = jnp.arange(width) // HEAD_DIM
    qbd = jnp.where(head_of_lane[None, None, None, :] == jnp.arange(n_heads)[None, None, :, None],
                    q3[:, :, None, :], jnp.zeros((), BF16)).reshape(nb, nrow, width)
    pad_new = 8 - n_new
    kn = jnp.pad(kn_bf.reshape(nb, n_new, width), ((0, 0), (0, pad_new), (0, 0)))
    vn = jnp.pad(vn_bf.reshape(nb, n_new, width), ((0, 0), (0, pad_new), (0, 0)))
    bias_rows = jnp.broadcast_to(jnp.tile(bias, n_new)[:, None], (nrow, pages * tok)).astype(F32)

    def page_spec(r):
        return pl.BlockSpec((None, 1, tok * n_heads, HEAD_DIM),
                            lambda b, s, pt: (layer, pt[b, npg - 1 - (s * pages + r)], 0, 0))

    const = lambda shape: pl.BlockSpec(shape, lambda b, s, pt: (0,) * len(shape))
    per_b = lambda shape: pl.BlockSpec(shape, lambda b, s, pt: (b,) + (0,) * (len(shape) - 1))
    grid_spec = pltpu.PrefetchScalarGridSpec(
        num_scalar_prefetch=1,
        grid=(nb, nstep),
        in_specs=[per_b((1, nrow, width)), per_b((1, 8, width)), per_b((1, 8, width)),
                  const((nrow, pages * tok)), const((tok, tok)), const((width, width)), const((1, width))]
                 + [page_spec(r) for r in range(pages)] * 2,
        out_specs=per_b((1, n_new, width)),
        scratch_shapes=[pltpu.VMEM((nrow, width), F32), pltpu.VMEM((nrow, tok), F32)],
    )
    return pl.pallas_call(
        functools.partial(_sb_sample_kernel, pages=pages, n_heads=n_heads, n_new=n_new),
        grid_spec=grid_spec,
        out_shape=jax.ShapeDtypeStruct((nb, n_new, width), BF16),
        compiler_params=_params("parallel", "arbitrary"),
        name="sb_sample",
    )(page_table, qbd, kn, vn, bias_rows, _tri_matrix(tok), bd_sb, g_row,
      *([pool_k.reshape(*pool_k.shape[:2], tok * n_heads, HEAD_DIM)] * pages),
      *([pool_v.reshape(*pool_v.shape[:2], tok * n_heads, HEAD_DIM)] * pages))


def _conv_kernel(*refs, ts, taps, chunk, has_prev):
    if has_prev:
        hist_ref, prev_ref, cur_ref, w_ref, b_ref, lg_ref, lb_ref, o_ref, buf = refs
        head = jnp.where(pl.program_id(1) == 0, hist_ref[0], prev_ref[0])
    else:
        hist_ref, cur_ref, w_ref, b_ref, lg_ref, lb_ref, o_ref, buf = refs
        head = hist_ref[0]
    buf[0:CONV_PAD, :] = head
    buf[CONV_PAD:CONV_PAD + ts, :] = cur_ref[0]
    first = CONV_PAD - (taps - 1)
    for c0 in range(0, ts, chunk):
        acc = jnp.zeros((chunk, buf.shape[1]), F32)
        for k in range(taps):
            acc = acc + w_ref[k:k + 1, :] * buf[first + c0 + k:first + c0 + k + chunk, :]
        c = acc + b_ref[...]
        xc = c - jnp.mean(c, axis=-1, keepdims=True)
        y = xc * lax.rsqrt(jnp.mean(xc * xc, axis=-1, keepdims=True) + EPS) * lg_ref[...] + lb_ref[...]
        o_ref[0, c0:c0 + chunk, :] = (y * jax.nn.sigmoid(y)).astype(BF16)


def _conv_module(glu3, hist_pad, w, b, lg, lb, *, ts):
    nb, s, cch = glu3.shape
    taps = w.shape[0]
    has_prev = s > ts
    chunk = min(ts, 64)
    cur = pl.BlockSpec((1, ts, cch), lambda bb, i: (bb, i, 0))
    hist = pl.BlockSpec((1, CONV_PAD, cch), lambda bb, i: (bb, 0, 0))
    const = lambda a: pl.BlockSpec(a.shape, lambda bb, i: (0,) * a.ndim)
    in_specs, args = [hist], [hist_pad]
    if has_prev:
        per = ts // CONV_PAD
        in_specs.append(pl.BlockSpec((1, CONV_PAD, cch), lambda bb, i: (bb, jnp.maximum(i * per - 1, 0), 0)))
        args.append(glu3)
    in_specs += [cur, const(w), const(b), const(lg), const(lb)]
    args += [glu3, w, b, lg, lb]
    return pl.pallas_call(
        functools.partial(_conv_kernel, ts=ts, taps=taps, chunk=chunk, has_prev=has_prev),
        grid=(nb, s // ts),
        in_specs=in_specs,
        out_specs=cur,
        out_shape=jax.ShapeDtypeStruct((nb, s, cch), BF16),
        scratch_shapes=[pltpu.VMEM((CONV_PAD + max(ts, 8), cch), F32)],
        compiler_params=_params("parallel", "parallel"),
        name="conv_module",
    )(*args)


def _memkv_kernel(m_ref, g_ref, wk_ref, wv_ref, bd_ref, gk_ref, k_ref, v_ref):
    x = m_ref[...]
    xn = (x * lax.rsqrt(jnp.mean(x * x, axis=-1, keepdims=True) + EPS) * g_ref[...]).astype(BF16)
    k = _mm(xn, wk_ref[...])
    k_ref[...] = k * lax.rsqrt(_group_mean_sq(k, bd_ref[...]) + EPS) * gk_ref[...]
    v_ref[...] = _mm(xn, wv_ref[...])


def _memkv(mem, g, wk_bf, wv_bf, bd_mem, gk, *, tm):
    t, d = mem.shape
    memw = wk_bf.shape[1]
    full = lambda a: pl.BlockSpec(a.shape, lambda i: (0,) * a.ndim)
    out = pl.BlockSpec((tm, memw), lambda i: (i, 0))
    return pl.pallas_call(
        _memkv_kernel,
        grid=(t // tm,),
        in_specs=[pl.BlockSpec((tm, d), lambda i: (i, 0)), full(g), full(wk_bf), full(wv_bf), full(bd_mem), full(gk)],
        out_specs=[out, out],
        out_shape=[jax.ShapeDtypeStruct((t, memw), F32)] * 2,
        compiler_params=_params("parallel"),
        name="memkv",
    )(mem, g, wk_bf, wv_bf, bd_mem, gk)


def _mem_attn_kernel(q_ref, k_ref, v_ref, bd_ref, g_ref, o_ref, *, n_heads):
    q = q_ref[0]
    tq = q.shape[0]
    rows = max(tq, 8)
    if rows != tq:
        q = jnp.concatenate([q, jnp.zeros((rows - tq, q.shape[1]), q.dtype)], axis=0)
    kb = k_ref[0].astype(BF16)
    vb = v_ref[0].astype(BF16)
    head = lax.broadcasted_iota(jnp.int32, q.shape, 1) // HEAD_DIM
    zero = jnp.zeros_like(q)
    out = jnp.zeros(q.shape, F32)
    for h in range(n_heads):
        s = _nt(jnp.where(head == h, q, zero), kb) * (HEAD_DIM ** -0.5)
        p = jnp.exp(s - jnp.max(s, axis=-1, keepdims=True))
        oh = _mm(p.astype(BF16), vb) / jnp.sum(p, axis=-1, keepdims=True)
        out = jnp.where(head == h, oh, out)
    out = out * lax.rsqrt(_group_mean_sq(out, bd_ref[...]) + EPS) * g_ref[...]
    o_ref[0] = out[:tq].astype(BF16)


def _mem_attn(q3, k3, v3, bd_mem, g_row, *, tq):
    ng, s, memw = q3.shape
    m = k3.shape[1]
    const = lambda a: pl.BlockSpec(a.shape, lambda gq, i: (0,) * a.ndim)
    qspec = pl.BlockSpec((1, tq, memw), lambda gq, i: (gq, i, 0))
    kspec = pl.BlockSpec((1, m, memw), lambda gq, i: (gq, 0, 0))
    return pl.pallas_call(
        functools.partial(_mem_attn_kernel, n_heads=memw // HEAD_DIM),
        grid=(ng, s // tq),
        in_specs=[qspec, kspec, kspec, const(bd_mem), const(g_row)],
        out_specs=qspec,
        out_shape=jax.ShapeDtypeStruct((ng, s, memw), BF16),
        compiler_params=_params("parallel", "parallel"),
        name="mem_attn",
    )(q3, k3, v3, bd_mem, g_row)


def _outproj_kernel(x_ref, sb_ref, cv_ref, mm_ref, w_ref, o_ref):
    a = sb_ref.shape[1]
    b = a + cv_ref.shape[1]
    y = _mm(sb_ref[...], w_ref[0:a, :]) + _mm(cv_ref[...], w_ref[a:b, :]) + _mm(mm_ref[...], w_ref[b:, :])
    o_ref[...] = x_ref[...] + y


def _outproj(x, o_sb, o_conv, o_mem, w_bf, *, tm):
    t, d = x.shape
    row = lambda a: pl.BlockSpec((tm, a.shape[1]), lambda i: (i, 0))
    return pl.pallas_call(
        _outproj_kernel,
        grid=(t // tm,),
        in_specs=[row(x), row(o_sb), row(o_conv), row(o_mem), pl.BlockSpec(w_bf.shape, lambda i: (0, 0))],
        out_specs=row(x),
        out_shape=jax.ShapeDtypeStruct((t, d), F32),
        compiler_params=_params("parallel"),
        name="outproj",
    )(x, o_sb, o_conv, o_mem, w_bf)


def _top_values(s, n, with_rank=False):
    tops = []
    cur = s
    rank = jnp.full(s.shape, float(n), F32)
    for k in range(n):
        m = jnp.max(cur, axis=0, keepdims=True)
        tops.append(m)
        hit = cur >= jnp.where(m == NEG_INF, POS_INF, m)
        if with_rank:
            rank = jnp.where(hit, float(k), rank)
        cur = jnp.where(hit, NEG_INF, cur)
    return (tops, rank) if with_rank else tops


def _peer_select_tile(s1, s2):
    t1 = jnp.concatenate(_top_values(s1, TOPK), axis=0)
    tops2, rank2 = _top_values(s2, TOPK, with_rank=True)
    t2 = jnp.concatenate(tops2, axis=0)
    row8 = lax.broadcasted_iota(jnp.int32, (8, LANES), 0)
    parts = [t1[0:1] + t2]
    for a in range(1, 8):
        parts.append(jnp.where(row8 < TOPK // (a + 1), t1[a:a + 1] + t2[0:8], NEG_INF))
    parts.append(t1[8:TOPK] + t2[0:1])
    best = _top_values(jnp.concatenate(parts, axis=0), TOPK)
    tau = best[TOPK - 1]
    top = best[0]
    zsum = jnp.zeros_like(top)
    for bv in best:
        zsum = zsum + jnp.exp(bv - top)
    cnt = jnp.zeros(s1.shape, F32)
    for a in range(TOPK):
        sel = (t1[a:a + 1] + t2) >= tau
        cnt_a = jnp.sum(jnp.where(sel, 1.0, 0.0), axis=0, keepdims=True)
        cnt = jnp.where(s1 == t1[a:a + 1], cnt_a, cnt)
    e1 = jnp.exp(s1 - t1[0:1]) / zsum
    e2 = jnp.exp(s2 - t2[0:1])
    return cnt, e1, rank2, e2


def _peer_select_kernel(x_ref, g_ref, wq_ref, keys_ref, xn_ref, cnt_ref, e1_ref, rk_ref, e2_ref,
                        qt_scr, s1_scr, s2_scr, *, n_heads):
    x = x_ref[...]
    xn = (x * lax.rsqrt(jnp.mean(x * x, axis=-1, keepdims=True) + EPS) * g_ref[...]).astype(BF16)
    xn_ref[...] = pltpu.bitcast(xn, jnp.uint32)
    qt_scr[...] = _nt(wq_ref[...], xn).astype(BF16)
    dh = keys_ref.shape[2]
    groups = x.shape[0] // LANES

    def head(h, c):
        r1 = pl.multiple_of(h * 2 * dh, 2 * dh)
        r2 = pl.multiple_of(h * 2 * dh + dh, dh)
        s1_scr[...] = _mm(keys_ref[2 * h], qt_scr[pl.ds(r1, dh), :])
        s2_scr[...] = _mm(keys_ref[2 * h + 1], qt_scr[pl.ds(r2, dh), :])
        for gi in range(groups):
            lanes = slice(gi * LANES, (gi + 1) * LANES)
            cnt, e1, rank2, e2 = _peer_select_tile(s1_scr[:, lanes], s2_scr[:, lanes])
            cnt_ref[h, gi] = cnt
            e1_ref[h, gi] = e1
            rk_ref[h, gi] = pltpu.bitcast(rank2.astype(BF16), jnp.uint32)
            e2_ref[h, gi] = pltpu.bitcast(e2.astype(BF16), jnp.uint32)
        return c

    lax.fori_loop(0, n_heads, head, 0)


def _peer_select(x, g, wq_t_bf, keys_bf, *, tm):
    t, d = x.shape
    n_heads = keys_bf.shape[0] // 2
    groups = tm // LANES
    full = lambda a: pl.BlockSpec(a.shape, lambda i: (0,) * a.ndim)
    sel = pl.BlockSpec((n_heads, groups, N_KEYS, LANES), lambda i: (0, i, 0, 0))
    row_shape = jax.ShapeDtypeStruct((n_heads, t // LANES, N_KEYS, LANES), F32)
    col = pl.BlockSpec((n_heads, groups, N_KEYS // 2, LANES), lambda i: (0, i, 0, 0))
    col_shape = jax.ShapeDtypeStruct((n_heads, t // LANES, N_KEYS // 2, LANES), jnp.uint32)
    return pl.pallas_call(
        functools.partial(_peer_select_kernel, n_heads=n_heads),
        grid=(t // tm,),
        in_specs=[pl.BlockSpec((tm, d), lambda i: (i, 0)), full(g), full(wq_t_bf), full(keys_bf)],
        out_specs=[pl.BlockSpec((tm // 2, d), lambda i: (i, 0)), sel, sel, col, col],
        out_shape=[jax.ShapeDtypeStruct((t // 2, d), jnp.uint32), row_shape, row_shape, col_shape, col_shape],
        scratch_shapes=[pltpu.VMEM((wq_t_bf.shape[0], tm), BF16), pltpu.VMEM((N_KEYS, tm), F32),
                        pltpu.VMEM((N_KEYS, tm), F32)],
        compiler_params=_params("parallel"),
        name="peer_select",
    )(x, g, wq_t_bf, keys_bf)


def _peer_dense_kernel(xn_ref, u_ref, vt_ref, cnt_ref, e1_ref, rk_ref, e2_ref, xmid_ref, o_ref,
                       wa_scr, acc_scr, *, splits):
    blk = pl.program_id(1)
    n_heads, groups = cnt_ref.shape[:2]
    eb = 2 * u_ref.shape[0]
    rows_i = eb // N_KEYS
    xn = pltpu.bitcast(xn_ref[...], BF16)
    contrib = None
    for s in range(splits):
        part = slice(s * eb // splits, (s + 1) * eb // splits)
        half = slice(s * eb // (2 * splits), (s + 1) * eb // (2 * splits))
        a_part = _nt(pltpu.bitcast(u_ref[half, :], BF16), xn)
        for r in range(rows_i // splits):
            i_row = blk * rows_i + s * (rows_i // splits) + r
            rows = slice(r * N_KEYS, (r + 1) * N_KEYS)
            out_rows = slice(part.start + r * N_KEYS, part.start + (r + 1) * N_KEYS)
            for gi in range(groups):
                lanes = slice(gi * LANES, (gi + 1) * LANES)
                a = a_part[rows, lanes].astype(BF16)
                act = 0.5 * a * (1.0 + lax.erf(a * INV_SQRT2))
                gate = jnp.zeros((N_KEYS, LANES), BF16)
                for h in range(n_heads):
                    cnt = jnp.broadcast_to(cnt_ref[h, gi, pl.ds(i_row, 1), :], (N_KEYS, LANES)).astype(BF16)
                    e1 = jnp.broadcast_to(e1_ref[h, gi, pl.ds(i_row, 1), :], (N_KEYS, LANES)).astype(BF16)
                    rk = pltpu.bitcast(rk_ref[h, gi], BF16)
                    e2 = pltpu.bitcast(e2_ref[h, gi], BF16)
                    gate = gate + jnp.where(rk < cnt, e2, jnp.zeros_like(e2)) * e1
                wa_scr[out_rows, lanes] = gate * act
        c = _mm(pltpu.bitcast(vt_ref[:, part], BF16), wa_scr[part, :])
        contrib = c if contrib is None else contrib + c

    @pl.when(blk == 0)
    def _():
        acc_scr[...] = contrib

    @pl.when(blk != 0)
    def _():
        acc_scr[...] += contrib

    @pl.when(blk == pl.num_programs(1) - 1)
    def _():
        o_ref[...] = xmid_ref[...] + acc_scr[...].T


def _peer_dense(xn_pk, u_pk, vt_pk, cnt, e1, rk, e2, x_mid, *, tm, eb):
    t, d = x_mid.shape
    n_blocks = 2 * u_pk.shape[0] // eb
    n_heads = cnt.shape[0]
    groups = tm // LANES
    tok = pl.BlockSpec((tm, d), lambda i, e: (i, 0))
    sel = pl.BlockSpec((n_heads, groups, N_KEYS, LANES), lambda i, e: (0, i, 0, 0))
    col = pl.BlockSpec((n_heads, groups, N_KEYS // 2, LANES), lambda i, e: (0, i, 0, 0))
    return pl.pallas_call(
        functools.partial(_peer_dense_kernel, splits=2),
        grid=(t // tm, n_blocks),
        in_specs=[pl.BlockSpec((tm // 2, d), lambda i, e: (i, 0)),
                  pl.BlockSpec((eb // 2, d), lambda i, e: (e, 0)),
                  pl.BlockSpec((d // 2, eb), lambda i, e: (0, e)),
                  sel, sel, col, col, tok],
        out_specs=tok,
        out_shape=jax.ShapeDtypeStruct((t, d), F32),
        scratch_shapes=[pltpu.VMEM((eb, tm), BF16), pltpu.VMEM((d, tm), F32)],
        compiler_params=_params("parallel", "arbitrary"),
        name="peer_dense",
    )(xn_pk, u_pk, vt_pk, cnt, e1, rk, e2, x_mid)


def _pack_kernel(x_ref, o_ref):
    o_ref[...] = pltpu.bitcast(x_ref[...].astype(BF16), jnp.uint32)


def _pack_bf16(x, *, tr):
    r, c = x.shape
    return pl.pallas_call(
        _pack_kernel,
        grid=(r // tr,),
        in_specs=[pl.BlockSpec((tr, c), lambda i: (i, 0))],
        out_specs=pl.BlockSpec((tr // 2, c), lambda i: (i, 0)),
        out_shape=jax.ShapeDtypeStruct((r // 2, c), jnp.uint32),
        compiler_params=_params("parallel"),
        name="pack_bf16",
    )(x)


def _block_diag(width):
    idx = jnp.arange(width) // HEAD_DIM
    return jnp.where(idx[:, None] == idx[None, :], 1.0 / HEAD_DIM, 0.0).astype(BF16)


def _tile(n, want):
    return want if n % want == 0 else n


def kernel(x_prompt, x_sample, mem_prompt, cache_sb_k, cache_sb_v, page_table, cache_mem_k, cache_mem_v,
           state_conv, g_mix, w_in, sb_bias, sb_out_g, conv_w, conv_b, conv_ln_g, conv_ln_b, g_mem, w_mem_k,
           w_mem_v, mem_q_g, mem_k_g, mem_out_g, w_out, g_ffn, peer_w_q, peer_sub_keys, peer_u, peer_v):
    depth = w_in.shape[0]
    batch, seq, d = x_prompt.shape
    nb, n_new, _ = x_sample.shape
    sb_heads = sb_bias.shape[1]
    sbw = sb_heads * HEAD_DIM
    cch = conv_w.shape[2]
    taps = conv_w.shape[1]
    mem_heads = mem_out_g.shape[1]
    memw = mem_heads * HEAD_DIM
    mem_tok = mem_prompt.shape[1]
    n_pool, page = cache_sb_k.shape[1:3]
    n_exp = peer_u.shape[1]

    bd_mem = _block_diag(memw)
    bd_sb = _block_diag(sbw)
    xp = x_prompt.reshape(batch * seq, d)
    xs = x_sample.reshape(nb * n_new, d)
    tp = _tile(batch * seq, 512)
    tsm = _tile(nb * n_new, 512)
    outs = [[] for _ in range(8)]

    for l in range(depth):
        row = lambda a: a[l].reshape(1, -1)
        w_in_bf = w_in[l].astype(BF16)
        bias2 = sb_bias[l] * LOG2E
        w_out_bf = w_out[l].astype(BF16)
        gq = jnp.tile(mem_q_g[l], mem_heads).reshape(1, memw)
        gk = jnp.tile(mem_k_g[l], mem_heads).reshape(1, memw)
        conv_args = (conv_w[l], row(conv_b), row(conv_ln_g), row(conv_ln_b))

        mk, mv = _memkv(mem_prompt.reshape(batch * mem_tok, d), row(g_mem), w_mem_k[l].astype(BF16),
                        w_mem_v[l].astype(BF16), bd_mem, gk, tm=_tile(batch * mem_tok, 512))

        q, k, v, kb, vb, glu, qm = _inproj(xp, row(g_mix), w_in_bf, bd_mem, gq, sbw=sbw, cch=cch, memw=memw, tm=tp)
        o_sb = _sb_prompt(q, kb, vb, bias2, row(sb_out_g), batch=batch, seq=seq,
                          bq=_tile(seq, 512), sub=256)
        glu3 = glu.reshape(batch, seq, cch)
        o_conv = _conv_module(glu3, jnp.zeros((batch, CONV_PAD, cch), F32), *conv_args, ts=_tile(seq, 256))
        o_mem = _mem_attn(qm.reshape(batch, seq, memw), mk.reshape(batch, mem_tok, memw),
                          mv.reshape(batch, mem_tok, memw), bd_mem, row(mem_out_g), tq=_tile(seq, 512))
        xp_mid = _outproj(xp, o_sb, o_conv.reshape(-1, cch), o_mem.reshape(-1, memw), w_out_bf, tm=tp)

        q_s, k_s, v_s, kb_s, vb_s, glu_s, qm_s = _inproj(xs, row(g_mix), w_in_bf, bd_mem, gq,
                                                        sbw=sbw, cch=cch, memw=memw, tm=tsm)
        npg = page_table.shape[1]
        o_sb_s = _sb_sample(q_s, kb_s, vb_s, bias2, row(sb_out_g), bd_sb,
                            cache_sb_k, cache_sb_v, page_table, layer=l, n_new=n_new,
                            pages=8 if npg % 8 == 0 else npg)
        glu_s3 = glu_s.reshape(nb, n_new, cch)
        hist_s = jnp.pad(state_conv[l], ((0, 0), (CONV_PAD - (taps - 1), 0), (0, 0)))
        o_conv_s = _conv_module(glu_s3, hist_s, *conv_args, ts=n_new)
        o_mem_s = _mem_attn(qm_s.reshape(nb, n_new, memw), cache_mem_k[l].reshape(nb, mem_tok, memw),
                            cache_mem_v[l].reshape(nb, mem_tok, memw), bd_mem, row(mem_out_g), tq=n_new)
        xs_mid = _outproj(xs, o_sb_s.reshape(-1, sbw), o_conv_s.reshape(-1, cch), o_mem_s.reshape(-1, memw),
                          w_out_bf, tm=tsm)

        wq_t = peer_w_q[l].T.astype(BF16)
        keys = peer_sub_keys[l].reshape(-1, N_KEYS, peer_sub_keys.shape[-1]).astype(BF16)
        u_pk = _pack_bf16(peer_u[l], tr=_tile(n_exp, 1024))
        vt_pk = _pack_bf16(peer_v[l].T, tr=64)
        new = []
        for x_mid, tm in ((xp_mid, tp), (xs_mid, tsm)):
            xn_pk, cnt, e1, rk, e2 = _peer_select(x_mid, row(g_ffn), wq_t, keys, tm=tm)
            new.append(_peer_dense(xn_pk, u_pk, vt_pk, cnt, e1, rk, e2, x_mid, tm=tm, eb=_tile(n_exp, 1024)))
        xp, xs = new

        hist_rows = taps - 1
        cs = jnp.concatenate([state_conv[l], glu_s3], axis=1)[:, -hist_rows:]
        for lst, val in zip(outs, (k.reshape(batch, seq, sb_heads, HEAD_DIM), v.reshape(batch, seq, sb_heads, HEAD_DIM),
                                   k_s.reshape(nb, n_new, sb_heads, HEAD_DIM), v_s.reshape(nb, n_new, sb_heads, HEAD_DIM),
                                   mk.reshape(batch, mem_tok, mem_heads, HEAD_DIM),
                                   mv.reshape(batch, mem_tok, mem_heads, HEAD_DIM),
                                   glu3[:, seq - hist_rows:], cs)):
            lst.append(val)

    return (xp.reshape(batch, seq, d), xs.reshape(nb, n_new, d), *[jnp.stack(o) for o in outs])
```

```python
import functools
import math

import jax
import jax.numpy as jnp
from jax import lax
from jax.experimental import pallas as pl
from jax.experimental.pallas import tpu as pltpu

F32 = jnp.float32
BF16 = jnp.bfloat16

EPS = 1e-6
HEAD_DIM = 64
LANES = 128
TOPK = 16
N_KEYS = 128
CONV_PAD = 32
VMEM_LIMIT = 56 * 1024 * 1024
NEG_INF = float("-inf")
POS_INF = float("inf")
INV_SQRT2 = 1.0 / math.sqrt(2.0)
LOG2E = 1.0 / math.log(2.0)


def _nt(a, b):
    return lax.dot_general(a, b, (((1,), (1,)), ((), ())), preferred_element_type=F32)


def _mm(a, b):
    return jnp.dot(a, b, preferred_element_type=F32)


def _group_mean_sq(x, bd):
    sq = x * x
    hi = sq.astype(BF16)
    lo = (sq - hi.astype(F32)).astype(BF16)
    return _mm(hi, bd) + _mm(lo, bd)


def _params(*sem):
    return pltpu.CompilerParams(dimension_semantics=sem, vmem_limit_bytes=VMEM_LIMIT)


def _inproj_kernel(x_ref, g_ref, w_ref, bd_ref, gq_ref,
                   q_ref, k_ref, v_ref, kb_ref, vb_ref, glu_ref, qm_ref, *, sbw, cch):
    x = x_ref[...]
    xn = x * lax.rsqrt(jnp.mean(x * x, axis=-1, keepdims=True) + EPS) * g_ref[...]
    z = _mm(xn.astype(BF16), w_ref[...])
    q_ref[...] = (z[:, :sbw] * (LOG2E * HEAD_DIM ** -0.5)).astype(BF16)
    k = z[:, sbw:2 * sbw]
    v = z[:, 2 * sbw:3 * sbw]
    k_ref[...] = k
    v_ref[...] = v
    kb_ref[...] = k.astype(BF16)
    vb_ref[...] = v.astype(BF16)
    o = 3 * sbw
    glu_ref[...] = z[:, o:o + cch] * jax.nn.sigmoid(z[:, o + cch:o + 2 * cch])
    qm = z[:, o + 2 * cch:]
    qm_ref[...] = (qm * lax.rsqrt(_group_mean_sq(qm, bd_ref[...]) + EPS) * gq_ref[...]).astype(BF16)


def _inproj(x, g, w_bf, bd_mem, gq, *, sbw, cch, memw, tm):
    t, d = x.shape
    cols = w_bf.shape[1]
    row = lambda width: pl.BlockSpec((tm, width), lambda i: (i, 0))
    full = lambda a: pl.BlockSpec(a.shape, lambda i: (0,) * a.ndim)
    return pl.pallas_call(
        functools.partial(_inproj_kernel, sbw=sbw, cch=cch),
        grid=(t // tm,),
        in_specs=[row(d), full(g), full(w_bf), full(bd_mem), full(gq)],
        out_specs=[row(sbw), row(sbw), row(sbw), row(sbw), row(sbw), row(cch), row(memw)],
        out_shape=[jax.ShapeDtypeStruct((t, sbw), BF16), jax.ShapeDtypeStruct((t, sbw), F32),
                   jax.ShapeDtypeStruct((t, sbw), F32), jax.ShapeDtypeStruct((t, sbw), BF16),
                   jax.ShapeDtypeStruct((t, sbw), BF16), jax.ShapeDtypeStruct((t, cch), F32),
                   jax.ShapeDtypeStruct((t, memw), BF16)],
        compiler_params=_params("parallel"),
        name="inproj",
    )(x, g, w_bf, bd_mem, gq)


def _sb_weights(z, tri, car, mask):
    ks, sub = z.shape[1], tri.shape[0]
    neg_abs = lax.bitcast_convert_type(lax.bitcast_convert_type(z, jnp.uint32) | jnp.uint32(0x80000000), F32)
    lb = jnp.minimum(z, 0.0) - jnp.log2(1.0 + jnp.exp2(neg_abs))
    l1m = lb - z
    if mask is not None:
        l1m = jnp.where(mask, l1m, 0.0)
    ws = [None] * (ks // sub)
    for j in reversed(range(ks // sub)):
        cols = slice(j * sub, (j + 1) * sub)
        lj = l1m[:, cols].astype(BF16)
        tail = _mm(lj, tri)
        w = jnp.exp2(lb[:, cols] + tail + car)
        if mask is not None:
            w = jnp.where(mask[:, cols], w, 0.0)
        ws[j] = w.astype(BF16)
        total = tail[:, 0:1] + lj[:, 0:1].astype(F32)
        car = car + jnp.broadcast_to(total, car.shape)
    return (ws[0] if len(ws) == 1 else jnp.concatenate(ws, axis=1)), car


def _sb_prompt_kernel(bias_ref, q_ref, k_ref, v_ref, tri_ref, g_ref, o_ref, acc, car_a, car_b, *, bq):
    hp = pl.program_id(1)
    qi = pl.program_id(2)
    q2 = q_ref[...]
    first = lax.broadcasted_iota(jnp.int32, (bq, LANES), 1) < HEAD_DIM
    zero = jnp.zeros_like(q2)
    qa = jnp.where(first, q2, zero)
    qb = jnp.where(first, zero, q2)
    bias_a = bias_ref[2 * hp]
    bias_b = bias_ref[2 * hp + 1]
    tri = tri_ref[...]
    for r in (acc, car_a, car_b):
        r[...] = jnp.zeros_like(r)

    def step(kb, mask):
        off = pl.multiple_of(kb * bq, bq)
        kblk = k_ref[pl.ds(off, bq), :]
        vblk = v_ref[pl.ds(off, bq), :]
        vzero = jnp.zeros_like(vblk)
        wa, car_a[...] = _sb_weights(_nt(qa, kblk) + bias_a, tri, car_a[...], mask)
        wb, car_b[...] = _sb_weights(_nt(qb, kblk) + bias_b, tri, car_b[...], mask)
        acc[...] += _mm(wa, jnp.where(first, vblk, vzero)) + _mm(wb, jnp.where(first, vzero, vblk))

    rows = lax.broadcasted_iota(jnp.int32, (bq, bq), 0)
    cols = lax.broadcasted_iota(jnp.int32, (bq, bq), 1)
    step(qi, cols < rows)

    def body(i, c):
        step(qi - 1 - i, None)
        return c

    lax.fori_loop(0, qi, body, 0)

    o = acc[...]
    sq = o * o
    ms_a = jnp.sum(jnp.where(first, sq, 0.0), axis=-1, keepdims=True)
    ms_b = jnp.sum(jnp.where(first, 0.0, sq), axis=-1, keepdims=True)
    ms = jnp.where(first, ms_a, ms_b) * (1.0 / HEAD_DIM)
    o_ref[...] = (o * lax.rsqrt(ms + EPS) * g_ref[...]).astype(BF16)


def _tri_matrix(n):
    return (jnp.arange(n)[:, None] > jnp.arange(n)[None, :]).astype(BF16)


def _sb_prompt(q_bf, k_bf, v_bf, bias, g_row, *, batch, seq, bq, sub):
    t, sbw = q_bf.shape
    npair = sbw // LANES
    nq = seq // bq
    return pl.pallas_call(
        functools.partial(_sb_prompt_kernel, bq=bq),
        grid=(batch, npair, nq),
        in_specs=[pl.BlockSpec(memory_space=pltpu.SMEM),
                  pl.BlockSpec((bq, LANES), lambda b, p, i: (b * nq + i, p)),
                  pl.BlockSpec((seq, LANES), lambda b, p, i: (b, p)),
                  pl.BlockSpec((seq, LANES), lambda b, p, i: (b, p)),
                  pl.BlockSpec((sub, sub), lambda b, p, i: (0, 0)),
                  pl.BlockSpec((1, LANES), lambda b, p, i: (0, p))],
        out_specs=pl.BlockSpec((bq, LANES), lambda b, p, i: (b * nq + i, p)),
        out_shape=jax.ShapeDtypeStruct((t, sbw), BF16),
        scratch_shapes=[pltpu.VMEM((bq, LANES), F32), pltpu.VMEM((bq, sub), F32), pltpu.VMEM((bq, sub), F32)],
        compiler_params=_params("parallel", "parallel", "arbitrary"),
        name="sb_prompt",
    )(bias, q_bf, k_bf, v_bf, _tri_matrix(sub), g_row)


def _sb_sample_kernel(pt_ref, q_ref, kn_ref, vn_ref, bias_ref, tri_ref, bd_ref, g_ref, *rest,
                      pages, n_heads, n_new):
    k_refs = rest[:pages]
    v_refs = rest[pages:2 * pages]
    o_ref = rest[2 * pages]
    acc, car = rest[2 * pages + 1:]
    s = pl.program_id(1)
    q = q_ref[0]
    bias = bias_ref[...]
    tri = tri_ref[...]
    bk = tri.shape[0]
    nrow, width = q.shape

    @pl.when(s == 0)
    def _():
        pad = jnp.zeros((bk - kn_ref.shape[1], width), BF16)
        kblk = jnp.concatenate([kn_ref[0], pad], axis=0)
        vblk = jnp.concatenate([vn_ref[0], pad], axis=0)
        rows = lax.broadcasted_iota(jnp.int32, (nrow, bk), 0)
        cols = lax.broadcasted_iota(jnp.int32, (nrow, bk), 1)
        mask = cols < rows // n_heads
        w, car[...] = _sb_weights(_nt(q, kblk) + bias[:, :bk], tri, jnp.zeros(car.shape, F32), mask)
        acc[...] = _mm(w, vblk)

    kt = jnp.concatenate([k_refs[r][0].astype(BF16) for r in reversed(range(pages))], axis=1)
    vt = jnp.concatenate([v_refs[r][0].astype(BF16) for r in reversed(range(pages))], axis=1)
    w, car[...] = _sb_weights(_mm(q, kt) + bias, tri, car[...], None)
    acc[...] += _nt(w, vt)

    @pl.when(s == pl.num_programs(1) - 1)
    def _():
        rows = lax.broadcasted_iota(jnp.int32, (nrow, width), 0)
        cols = lax.broadcasted_iota(jnp.int32, (nrow, width), 1)
        own = jnp.where(cols // HEAD_DIM == rows % n_heads, acc[...], 0.0)
        o = jnp.sum(own.reshape(n_new, n_heads, width), axis=1)
        o = o * lax.rsqrt(_group_mean_sq(o, bd_ref[...]) + EPS) * g_ref[...]
        o_ref[0] = o.astype(BF16)


def _sb_sample(q_bf, kn_bf, vn_bf, bias, g_row, bd_sb, pool_k, pool_v, page_table, *, layer, n_new, pages):
    nb, npg = page_table.shape
    tok, n_heads = pool_k.shape[2:4]
    width = n_heads * HEAD_DIM
    nrow = n_new * n_heads
    nstep = npg // pages
    q3 = q_bf.reshape(nb, n_new, width)
    head_of_lane = jnp.arange(width) // HEAD_DIM
    qbd = jnp.where(head_of_lane[None, None, None, :] == jnp.arange(n_heads)[None, None, :, None],
                    q3[:, :, None, :], jnp.zeros((), BF16)).reshape(nb, nrow, width)
    pad_new = 8 - n_new
    kn = jnp.pad(kn_bf.reshape(nb, n_new, width), ((0, 0), (0, pad_new), (0, 0)))
    vn = jnp.pad(vn_bf.reshape(nb, n_new, width), ((0, 0), (0, pad_new), (0, 0)))
    bias_rows = jnp.broadcast_to(jnp.tile(bias, n_new)[:, None], (nrow, pages * tok)).astype(F32)

    def token_minor(pool):
        return jnp.transpose(pool, (0, 1, 3, 4, 2)).reshape(*pool.shape[:2], width, tok)

    def page_spec(r):
        return pl.BlockSpec((None, 1, width, tok),
                            lambda b, s, pt: (layer, pt[b, npg - 1 - (s * pages + r)], 0, 0))

    const = lambda shape: pl.BlockSpec(shape, lambda b, s, pt: (0,) * len(shape))
    per_b = lambda shape: pl.BlockSpec(shape, lambda b, s, pt: (b,) + (0,) * (len(shape) - 1))
    grid_spec = pltpu.PrefetchScalarGridSpec(
        num_scalar_prefetch=1,
        grid=(nb, nstep),
        in_specs=[per_b((1, nrow, width)), per_b((1, 8, width)), per_b((1, 8, width)),
                  const((nrow, pages * tok)), const((tok, tok)), const((width, width)), const((1, width))]
                 + [page_spec(r) for r in range(pages)] * 2,
        out_specs=per_b((1, n_new, width)),
        scratch_shapes=[pltpu.VMEM((nrow, width), F32), pltpu.VMEM((nrow, tok), F32)],
    )
    return pl.pallas_call(
        functools.partial(_sb_sample_kernel, pages=pages, n_heads=n_heads, n_new=n_new),
        grid_spec=grid_spec,
        out_shape=jax.ShapeDtypeStruct((nb, n_new, width), BF16),
        compiler_params=_params("parallel", "arbitrary"),
        name="sb_sample",
    )(page_table, qbd, kn, vn, bias_rows, _tri_matrix(tok), bd_sb, g_row,
      *([token_minor(pool_k)] * pages), *([token_minor(pool_v)] * pages))


def _conv_kernel(*refs, ts, taps, chunk, has_prev):
    if has_prev:
        hist_ref, prev_ref, cur_ref, w_ref, b_ref, lg_ref, lb_ref, o_ref, buf = refs
        head = jnp.where(pl.program_id(1) == 0, hist_ref[0], prev_ref[0])
    else:
        hist_ref, cur_ref, w_ref, b_ref, lg_ref, lb_ref, o_ref, buf = refs
        head = hist_ref[0]
    buf[0:CONV_PAD, :] = head
    buf[CONV_PAD:CONV_PAD + ts, :] = cur_ref[0]
    first = CONV_PAD - (taps - 1)
    for c0 in range(0, ts, chunk):
        acc = jnp.zeros((chunk, buf.shape[1]), F32)
        for k in range(taps):
            acc = acc + w_ref[k:k + 1, :] * buf[first + c0 + k:first + c0 + k + chunk, :]
        c = acc + b_ref[...]
        xc = c - jnp.mean(c, axis=-1, keepdims=True)
        y = xc * lax.rsqrt(jnp.mean(xc * xc, axis=-1, keepdims=True) + EPS) * lg_ref[...] + lb_ref[...]
        o_ref[0, c0:c0 + chunk, :] = (y * jax.nn.sigmoid(y)).astype(BF16)


def _conv_module(glu3, hist_pad, w, b, lg, lb, *, ts):
    nb, s, cch = glu3.shape
    taps = w.shape[0]
    has_prev = s > ts
    chunk = min(ts, 64)
    cur = pl.BlockSpec((1, ts, cch), lambda bb, i: (bb, i, 0))
    hist = pl.BlockSpec((1, CONV_PAD, cch), lambda bb, i: (bb, 0, 0))
    const = lambda a: pl.BlockSpec(a.shape, lambda bb, i: (0,) * a.ndim)
    in_specs, args = [hist], [hist_pad]
    if has_prev:
        per = ts // CONV_PAD
        in_specs.append(pl.BlockSpec((1, CONV_PAD, cch), lambda bb, i: (bb, jnp.maximum(i * per - 1, 0), 0)))
        args.append(glu3)
    in_specs += [cur, const(w), const(b), const(lg), const(lb)]
    args += [glu3, w, b, lg, lb]
    return pl.pallas_call(
        functools.partial(_conv_kernel, ts=ts, taps=taps, chunk=chunk, has_prev=has_prev),
        grid=(nb, s // ts),
        in_specs=in_specs,
        out_specs=cur,
        out_shape=jax.ShapeDtypeStruct((nb, s, cch), BF16),
        scratch_shapes=[pltpu.VMEM((CONV_PAD + max(ts, 8), cch), F32)],
        compiler_params=_params("parallel", "parallel"),
        name="conv_module",
    )(*args)


def _memkv_kernel(m_ref, g_ref, wk_ref, wv_ref, bd_ref, gk_ref, k_ref, v_ref):
    x = m_ref[...]
    xn = (x * lax.rsqrt(jnp.mean(x * x, axis=-1, keepdims=True) + EPS) * g_ref[...]).astype(BF16)
    k = _mm(xn, wk_ref[...])
    k_ref[...] = k * lax.rsqrt(_group_mean_sq(k, bd_ref[...]) + EPS) * gk_ref[...]
    v_ref[...] = _mm(xn, wv_ref[...])


def _memkv(mem, g, wk_bf, wv_bf, bd_mem, gk, *, tm):
    t, d = mem.shape
    memw = wk_bf.shape[1]
    full = lambda a: pl.BlockSpec(a.shape, lambda i: (0,) * a.ndim)
    out = pl.BlockSpec((tm, memw), lambda i: (i, 0))
    return pl.pallas_call(
        _memkv_kernel,
        grid=(t // tm,),
        in_specs=[pl.BlockSpec((tm, d), lambda i: (i, 0)), full(g), full(wk_bf), full(wv_bf), full(bd_mem), full(gk)],
        out_specs=[out, out],
        out_shape=[jax.ShapeDtypeStruct((t, memw), F32)] * 2,
        compiler_params=_params("parallel"),
        name="memkv",
    )(mem, g, wk_bf, wv_bf, bd_mem, gk)


def _mem_attn_kernel(q_ref, k_ref, v_ref, bd_ref, g_ref, o_ref, *, n_heads):
    q = q_ref[0]
    tq = q.shape[0]
    rows = max(tq, 8)
    if rows != tq:
        q = jnp.concatenate([q, jnp.zeros((rows - tq, q.shape[1]), q.dtype)], axis=0)
    kb = k_ref[0].astype(BF16)
    vb = v_ref[0].astype(BF16)
    head = lax.broadcasted_iota(jnp.int32, q.shape, 1) // HEAD_DIM
    zero = jnp.zeros_like(q)
    out = jnp.zeros(q.shape, F32)
    for h in range(n_heads):
        s = _nt(jnp.where(head == h, q, zero), kb) * (HEAD_DIM ** -0.5)
        p = jnp.exp(s - jnp.max(s, axis=-1, keepdims=True))
        oh = _mm(p.astype(BF16), vb) / jnp.sum(p, axis=-1, keepdims=True)
        out = jnp.where(head == h, oh, out)
    out = out * lax.rsqrt(_group_mean_sq(out, bd_ref[...]) + EPS) * g_ref[...]
    o_ref[0] = out[:tq].astype(BF16)


def _mem_attn(q3, k3, v3, bd_mem, g_row, *, tq):
    ng, s, memw = q3.shape
    m = k3.shape[1]
    const = lambda a: pl.BlockSpec(a.shape, lambda gq, i: (0,) * a.ndim)
    qspec = pl.BlockSpec((1, tq, memw), lambda gq, i: (gq, i, 0))
    kspec = pl.BlockSpec((1, m, memw), lambda gq, i: (gq, 0, 0))
    return pl.pallas_call(
        functools.partial(_mem_attn_kernel, n_heads=memw // HEAD_DIM),
        grid=(ng, s // tq),
        in_specs=[qspec, kspec, kspec, const(bd_mem), const(g_row)],
        out_specs=qspec,
        out_shape=jax.ShapeDtypeStruct((ng, s, memw), BF16),
        compiler_params=_params("parallel", "parallel"),
        name="mem_attn",
    )(q3, k3, v3, bd_mem, g_row)


def _outproj_kernel(x_ref, sb_ref, cv_ref, mm_ref, w_ref, o_ref):
    a = sb_ref.shape[1]
    b = a + cv_ref.shape[1]
    y = _mm(sb_ref[...], w_ref[0:a, :]) + _mm(cv_ref[...], w_ref[a:b, :]) + _mm(mm_ref[...], w_ref[b:, :])
    o_ref[...] = x_ref[...] + y


def _outproj(x, o_sb, o_conv, o_mem, w_bf, *, tm):
    t, d = x.shape
    row = lambda a: pl.BlockSpec((tm, a.shape[1]), lambda i: (i, 0))
    return pl.pallas_call(
        _outproj_kernel,
        grid=(t // tm,),
        in_specs=[row(x), row(o_sb), row(o_conv), row(o_mem), pl.BlockSpec(w_bf.shape, lambda i: (0, 0))],
        out_specs=row(x),
        out_shape=jax.ShapeDtypeStruct((t, d), F32),
        compiler_params=_params("parallel"),
        name="outproj",
    )(x, o_sb, o_conv, o_mem, w_bf)


def _top_values(s, n, with_rank=False):
    tops = []
    cur = s
    rank = jnp.full(s.shape, float(n), F32)
    for k in range(n):
        m = jnp.max(cur, axis=0, keepdims=True)
        tops.append(m)
        hit = cur >= jnp.where(m == NEG_INF, POS_INF, m)
        if with_rank:
            rank = jnp.where(hit, float(k), rank)
        cur = jnp.where(hit, NEG_INF, cur)
    return (tops, rank) if with_rank else tops


def _peer_select_tile(s1, s2):
    t1 = jnp.concatenate(_top_values(s1, TOPK), axis=0)
    tops2, rank2 = _top_values(s2, TOPK, with_rank=True)
    t2 = jnp.concatenate(tops2, axis=0)
    row8 = lax.broadcasted_iota(jnp.int32, (8, LANES), 0)
    parts = [t1[0:1] + t2]
    for a in range(1, 8):
        parts.append(jnp.where(row8 < TOPK // (a + 1), t1[a:a + 1] + t2[0:8], NEG_INF))
    parts.append(t1[8:TOPK] + t2[0:1])
    best = _top_values(jnp.concatenate(parts, axis=0), TOPK)
    tau = best[TOPK - 1]
    top = best[0]
    zsum = jnp.zeros_like(top)
    for bv in best:
        zsum = zsum + jnp.exp(bv - top)
    cnt = jnp.zeros(s1.shape, F32)
    for a in range(TOPK):
        sel = (t1[a:a + 1] + t2) >= tau
        cnt_a = jnp.sum(jnp.where(sel, 1.0, 0.0), axis=0, keepdims=True)
        cnt = jnp.where(s1 == t1[a:a + 1], cnt_a, cnt)
    e1 = jnp.exp(s1 - t1[0:1]) / zsum
    e2 = jnp.exp(s2 - t2[0:1])
    return cnt, e1, rank2, e2


def _peer_select_kernel(x_ref, g_ref, wq_ref, keys_ref, xn_ref, cnt_ref, e1_ref, rk_ref, e2_ref,
                        qt_scr, s1_scr, s2_scr, *, n_heads):
    x = x_ref[...]
    xn = (x * lax.rsqrt(jnp.mean(x * x, axis=-1, keepdims=True) + EPS) * g_ref[...]).astype(BF16)
    xn_ref[...] = pltpu.bitcast(xn, jnp.uint32)
    qt_scr[...] = _nt(wq_ref[...], xn).astype(BF16)
    dh = keys_ref.shape[2]
    groups = x.shape[0] // LANES

    def head(h, c):
        r1 = pl.multiple_of(h * 2 * dh, 2 * dh)
        r2 = pl.multiple_of(h * 2 * dh + dh, dh)
        s1_scr[...] = _mm(keys_ref[2 * h], qt_scr[pl.ds(r1, dh), :])
        s2_scr[...] = _mm(keys_ref[2 * h + 1], qt_scr[pl.ds(r2, dh), :])
        for gi in range(groups):
            lanes = slice(gi * LANES, (gi + 1) * LANES)
            cnt, e1, rank2, e2 = _peer_select_tile(s1_scr[:, lanes], s2_scr[:, lanes])
            cnt_ref[h, gi] = cnt
            e1_ref[h, gi] = e1
            rk_ref[h, gi] = pltpu.bitcast(rank2.astype(BF16), jnp.uint32)
            e2_ref[h, gi] = pltpu.bitcast(e2.astype(BF16), jnp.uint32)
        return c

    lax.fori_loop(0, n_heads, head, 0)


def _peer_select(x, g, wq_t_bf, keys_bf, *, tm):
    t, d = x.shape
    n_heads = keys_bf.shape[0] // 2
    groups = tm // LANES
    full = lambda a: pl.BlockSpec(a.shape, lambda i: (0,) * a.ndim)
    sel = pl.BlockSpec((n_heads, groups, N_KEYS, LANES), lambda i: (0, i, 0, 0))
    row_shape = jax.ShapeDtypeStruct((n_heads, t // LANES, N_KEYS, LANES), F32)
    col = pl.BlockSpec((n_heads, groups, N_KEYS // 2, LANES), lambda i: (0, i, 0, 0))
    col_shape = jax.ShapeDtypeStruct((n_heads, t // LANES, N_KEYS // 2, LANES), jnp.uint32)
    return pl.pallas_call(
        functools.partial(_peer_select_kernel, n_heads=n_heads),
        grid=(t // tm,),
        in_specs=[pl.BlockSpec((tm, d), lambda i: (i, 0)), full(g), full(wq_t_bf), full(keys_bf)],
        out_specs=[pl.BlockSpec((tm // 2, d), lambda i: (i, 0)), sel, sel, col, col],
        out_shape=[jax.ShapeDtypeStruct((t // 2, d), jnp.uint32), row_shape, row_shape, col_shape, col_shape],
        scratch_shapes=[pltpu.VMEM((wq_t_bf.shape[0], tm), BF16), pltpu.VMEM((N_KEYS, tm), F32),
                        pltpu.VMEM((N_KEYS, tm), F32)],
        compiler_params=_params("parallel"),
        name="peer_select",
    )(x, g, wq_t_bf, keys_bf)


def _peer_dense_kernel(xn_ref, u_ref, vt_ref, cnt_ref, e1_ref, rk_ref, e2_ref, xmid_ref, o_ref,
                       wa_scr, acc_scr, *, splits):
    blk = pl.program_id(1)
    n_heads, groups = cnt_ref.shape[:2]
    eb = 2 * u_ref.shape[0]
    rows_i = eb // N_KEYS
    xn = pltpu.bitcast(xn_ref[...], BF16)
    contrib = None
    for s in range(splits):
        part = slice(s * eb // splits, (s + 1) * eb // splits)
        half = slice(s * eb // (2 * splits), (s + 1) * eb // (2 * splits))
        a_part = _nt(pltpu.bitcast(u_ref[half, :], BF16), xn)
        for r in range(rows_i // splits):
            i_row = blk * rows_i + s * (rows_i // splits) + r
            rows = slice(r * N_KEYS, (r + 1) * N_KEYS)
            out_rows = slice(part.start + r * N_KEYS, part.start + (r + 1) * N_KEYS)
            for gi in range(groups):
                lanes = slice(gi * LANES, (gi + 1) * LANES)
                a = a_part[rows, lanes].astype(BF16)
                act = 0.5 * a * (1.0 + lax.erf(a * INV_SQRT2))
                gate = jnp.zeros((N_KEYS, LANES), BF16)
                for h in range(n_heads):
                    cnt = jnp.broadcast_to(cnt_ref[h, gi, pl.ds(i_row, 1), :], (N_KEYS, LANES)).astype(BF16)
                    e1 = jnp.broadcast_to(e1_ref[h, gi, pl.ds(i_row, 1), :], (N_KEYS, LANES)).astype(BF16)
                    rk = pltpu.bitcast(rk_ref[h, gi], BF16)
                    e2 = pltpu.bitcast(e2_ref[h, gi], BF16)
                    gate = gate + jnp.where(rk < cnt, e2, jnp.zeros_like(e2)) * e1
                wa_scr[out_rows, lanes] = gate * act
        c = _mm(pltpu.bitcast(vt_ref[:, part], BF16), wa_scr[part, :])
        contrib = c if contrib is None else contrib + c

    @pl.when(blk == 0)
    def _():
        acc_scr[...] = contrib

    @pl.when(blk != 0)
    def _():
        acc_scr[...] += contrib

    @pl.when(blk == pl.num_programs(1) - 1)
    def _():
        o_ref[...] = xmid_ref[...] + acc_scr[...].T


def _peer_dense(xn_pk, u_pk, vt_pk, cnt, e1, rk, e2, x_mid, *, tm, eb):
    t, d = x_mid.shape
    n_blocks = 2 * u_pk.shape[0] // eb
    n_heads = cnt.shape[0]
    groups = tm // LANES
    tok = pl.BlockSpec((tm, d), lambda i, e: (i, 0))
    sel = pl.BlockSpec((n_heads, groups, N_KEYS, LANES), lambda i, e: (0, i, 0, 0))
    col = pl.BlockSpec((n_heads, groups, N_KEYS // 2, LANES), lambda i, e: (0, i, 0, 0))
    return pl.pallas_call(
        functools.partial(_peer_dense_kernel, splits=2),
        grid=(t // tm, n_blocks),
        in_specs=[pl.BlockSpec((tm // 2, d), lambda i, e: (i, 0)),
                  pl.BlockSpec((eb // 2, d), lambda i, e: (e, 0)),
                  pl.BlockSpec((d // 2, eb), lambda i, e: (0, e)),
                  sel, sel, col, col, tok],
        out_specs=tok,
        out_shape=jax.ShapeDtypeStruct((t, d), F32),
        scratch_shapes=[pltpu.VMEM((eb, tm), BF16), pltpu.VMEM((d, tm), F32)],
        compiler_params=_params("parallel", "arbitrary"),
        name="peer_dense",
    )(xn_pk, u_pk, vt_pk, cnt, e1, rk, e2, x_mid)


def _pack_kernel(x_ref, o_ref):
    o_ref[...] = pltpu.bitcast(x_ref[...].astype(BF16), jnp.uint32)


def _pack_bf16(x, *, tr):
    r, c = x.shape
    return pl.pallas_call(
        _pack_kernel,
        grid=(r // tr,),
        in_specs=[pl.BlockSpec((tr, c), lambda i: (i, 0))],
        out_specs=pl.BlockSpec((tr // 2, c), lambda i: (i, 0)),
        out_shape=jax.ShapeDtypeStruct((r // 2, c), jnp.uint32),
        compiler_params=_params("parallel"),
        name="pack_bf16",
    )(x)


def _block_diag(width):
    idx = jnp.arange(width) // HEAD_DIM
    return jnp.where(idx[:, None] == idx[None, :], 1.0 / HEAD_DIM, 0.0).astype(BF16)


def _tile(n, want):
    return want if n % want == 0 else n


def kernel(x_prompt, x_sample, mem_prompt, cache_sb_k, cache_sb_v, page_table, cache_mem_k, cache_mem_v,
           state_conv, g_mix, w_in, sb_bias, sb_out_g, conv_w, conv_b, conv_ln_g, conv_ln_b, g_mem, w_mem_k,
           w_mem_v, mem_q_g, mem_k_g, mem_out_g, w_out, g_ffn, peer_w_q, peer_sub_keys, peer_u, peer_v):
    depth = w_in.shape[0]
    batch, seq, d = x_prompt.shape
    nb, n_new, _ = x_sample.shape
    sb_heads = sb_bias.shape[1]
    sbw = sb_heads * HEAD_DIM
    cch = conv_w.shape[2]
    taps = conv_w.shape[1]
    mem_heads = mem_out_g.shape[1]
    memw = mem_heads * HEAD_DIM
    mem_tok = mem_prompt.shape[1]
    n_pool, page = cache_sb_k.shape[1:3]
    n_exp = peer_u.shape[1]

    bd_mem = _block_diag(memw)
    bd_sb = _block_diag(sbw)
    xp = x_prompt.reshape(batch * seq, d)
    xs = x_sample.reshape(nb * n_new, d)
    tp = _tile(batch * seq, 512)
    tsm = _tile(nb * n_new, 512)
    outs = [[] for _ in range(8)]

    for l in range(depth):
        row = lambda a: a[l].reshape(1, -1)
        w_in_bf = w_in[l].astype(BF16)
        bias2 = sb_bias[l] * LOG2E
        w_out_bf = w_out[l].astype(BF16)
        gq = jnp.tile(mem_q_g[l], mem_heads).reshape(1, memw)
        gk = jnp.tile(mem_k_g[l], mem_heads).reshape(1, memw)
        conv_args = (conv_w[l], row(conv_b), row(conv_ln_g), row(conv_ln_b))

        mk, mv = _memkv(mem_prompt.reshape(batch * mem_tok, d), row(g_mem), w_mem_k[l].astype(BF16),
                        w_mem_v[l].astype(BF16), bd_mem, gk, tm=_tile(batch * mem_tok, 512))

        q, k, v, kb, vb, glu, qm = _inproj(xp, row(g_mix), w_in_bf, bd_mem, gq, sbw=sbw, cch=cch, memw=memw, tm=tp)
        o_sb = _sb_prompt(q, kb, vb, bias2, row(sb_out_g), batch=batch, seq=seq,
                          bq=_tile(seq, 512), sub=256)
        glu3 = glu.reshape(batch, seq, cch)
        o_conv = _conv_module(glu3, jnp.zeros((batch, CONV_PAD, cch), F32), *conv_args, ts=_tile(seq, 256))
        o_mem = _mem_attn(qm.reshape(batch, seq, memw), mk.reshape(batch, mem_tok, memw),
                          mv.reshape(batch, mem_tok, memw), bd_mem, row(mem_out_g), tq=_tile(seq, 512))
        xp_mid = _outproj(xp, o_sb, o_conv.reshape(-1, cch), o_mem.reshape(-1, memw), w_out_bf, tm=tp)

        q_s, k_s, v_s, kb_s, vb_s, glu_s, qm_s = _inproj(xs, row(g_mix), w_in_bf, bd_mem, gq,
                                                        sbw=sbw, cch=cch, memw=memw, tm=tsm)
        npg = page_table.shape[1]
        o_sb_s = _sb_sample(q_s, kb_s, vb_s, bias2, row(sb_out_g), bd_sb,
                            cache_sb_k, cache_sb_v, page_table, layer=l, n_new=n_new,
                            pages=8 if npg % 8 == 0 else npg)
        glu_s3 = glu_s.reshape(nb, n_new, cch)
        hist_s = jnp.pad(state_conv[l], ((0, 0), (CONV_PAD - (taps - 1), 0), (0, 0)))
        o_conv_s = _conv_module(glu_s3, hist_s, *conv_args, ts=n_new)
        o_mem_s = _mem_attn(qm_s.reshape(nb, n_new, memw), cache_mem_k[l].reshape(nb, mem_tok, memw),
                            cache_mem_v[l].reshape(nb, mem_tok, memw), bd_mem, row(mem_out_g), tq=n_new)
        xs_mid = _outproj(xs, o_sb_s.reshape(-1, sbw), o_conv_s.reshape(-1, cch), o_mem_s.reshape(-1, memw),
                          w_out_bf, tm=tsm)

        wq_t = peer_w_q[l].T.astype(BF16)
        keys = peer_sub_keys[l].reshape(-1, N_KEYS, peer_sub_keys.shape[-1]).astype(BF16)
        u_pk = _pack_bf16(peer_u[l], tr=_tile(n_exp, 1024))
        vt_pk = _pack_bf16(peer_v[l].T, tr=64)
        new = []
        for x_mid, tm in ((xp_mid, tp), (xs_mid, tsm)):
            xn_pk, cnt, e1, rk, e2 = _peer_select(x_mid, row(g_ffn), wq_t, keys, tm=tm)
            new.append(_peer_dense(xn_pk, u_pk, vt_pk, cnt, e1, rk, e2, x_mid, tm=tm, eb=_tile(n_exp, 1024)))
        xp, xs = new

        hist_rows = taps - 1
        cs = jnp.concatenate([state_conv[l], glu_s3], axis=1)[:, -hist_rows:]
        for lst, val in zip(outs, (k.reshape(batch, seq, sb_heads, HEAD_DIM), v.reshape(batch, seq, sb_heads, HEAD_DIM),
                                   k_s.reshape(nb, n_new, sb_heads, HEAD_DIM), v_s.reshape(nb, n_new, sb_heads, HEAD_DIM),
                                   mk.reshape(batch, mem_tok, mem_heads, HEAD_DIM),
                                   mv.reshape(batch, mem_tok, mem_heads, HEAD_DIM),
                                   glu3[:, seq - hist_rows:], cs)):
            lst.append(val)

    return (xp.reshape(batch, seq, d), xs.reshape(nb, n_new, d), *[jnp.stack(o) for o in outs])
```

```python
import functools
import math

import jax
import jax.numpy as jnp
from jax import lax
from jax.experimental import pallas as pl
from jax.experimental.pallas import tpu as pltpu

F32 = jnp.float32
BF16 = jnp.bfloat16

EPS = 1e-6
HEAD_DIM = 64
LANES = 128
TOPK = 16
N_KEYS = 128
CONV_PAD = 32
VMEM_LIMIT = 56 * 1024 * 1024
NEG_INF = float("-inf")
POS_INF = float("inf")
INV_SQRT2 = 1.0 / math.sqrt(2.0)
LOG2E = 1.0 / math.log(2.0)


def _nt(a, b):
    return lax.dot_general(a, b, (((1,), (1,)), ((), ())), preferred_element_type=F32)


def _mm(a, b):
    return jnp.dot(a, b, preferred_element_type=F32)


def _group_mean_sq(x, bd):
    sq = x * x
    hi = sq.astype(BF16)
    lo = (sq - hi.astype(F32)).astype(BF16)
    return _mm(hi, bd) + _mm(lo, bd)


def _params(*sem):
    return pltpu.CompilerParams(dimension_semantics=sem, vmem_limit_bytes=VMEM_LIMIT)


def _inproj_kernel(x_ref, g_ref, w_ref, bd_ref, gq_ref,
                   q_ref, k_ref, v_ref, kb_ref, vb_ref, glu_ref, qm_ref, *, sbw, cch):
    x = x_ref[...]
    xn = x * lax.rsqrt(jnp.mean(x * x, axis=-1, keepdims=True) + EPS) * g_ref[...]
    z = _mm(xn.astype(BF16), w_ref[...])
    q_ref[...] = (z[:, :sbw] * (LOG2E * HEAD_DIM ** -0.5)).astype(BF16)
    k = z[:, sbw:2 * sbw]
    v = z[:, 2 * sbw:3 * sbw]
    k_ref[...] = k
    v_ref[...] = v
    kb_ref[...] = k.astype(BF16)
    vb_ref[...] = v.astype(BF16)
    o = 3 * sbw
    glu_ref[...] = z[:, o:o + cch] * jax.nn.sigmoid(z[:, o + cch:o + 2 * cch])
    qm = z[:, o + 2 * cch:]
    qm_ref[...] = (qm * lax.rsqrt(_group_mean_sq(qm, bd_ref[...]) + EPS) * gq_ref[...]).astype(BF16)


def _inproj(x, g, w_bf, bd_mem, gq, *, sbw, cch, memw, tm):
    t, d = x.shape
    cols = w_bf.shape[1]
    row = lambda width: pl.BlockSpec((tm, width), lambda i: (i, 0))
    full = lambda a: pl.BlockSpec(a.shape, lambda i: (0,) * a.ndim)
    return pl.pallas_call(
        functools.partial(_inproj_kernel, sbw=sbw, cch=cch),
        grid=(t // tm,),
        in_specs=[row(d), full(g), full(w_bf), full(bd_mem), full(gq)],
        out_specs=[row(sbw), row(sbw), row(sbw), row(sbw), row(sbw), row(cch), row(memw)],
        out_shape=[jax.ShapeDtypeStruct((t, sbw), BF16), jax.ShapeDtypeStruct((t, sbw), F32),
                   jax.ShapeDtypeStruct((t, sbw), F32), jax.ShapeDtypeStruct((t, sbw), BF16),
                   jax.ShapeDtypeStruct((t, sbw), BF16), jax.ShapeDtypeStruct((t, cch), F32),
                   jax.ShapeDtypeStruct((t, memw), BF16)],
        compiler_params=_params("parallel"),
        name="inproj",
    )(x, g, w_bf, bd_mem, gq)


def _sb_weights(z, tri, car, mask):
    ks, sub = z.shape[1], tri.shape[0]
    neg_abs = lax.bitcast_convert_type(lax.bitcast_convert_type(z, jnp.uint32) | jnp.uint32(0x80000000), F32)
    lb = jnp.minimum(z, 0.0) - jnp.log2(1.0 + jnp.exp2(neg_abs))
    l1m = lb - z
    if mask is not None:
        l1m = jnp.where(mask, l1m, 0.0)
    ws = [None] * (ks // sub)
    for j in reversed(range(ks // sub)):
        cols = slice(j * sub, (j + 1) * sub)
        lj = l1m[:, cols].astype(BF16)
        tail = _mm(lj, tri)
        w = jnp.exp2(lb[:, cols] + tail + jnp.tile(car, (1, sub // car.shape[1])))
        if mask is not None:
            w = jnp.where(mask[:, cols], w, 0.0)
        ws[j] = w.astype(BF16)
        total = tail[:, 0:1] + lj[:, 0:1].astype(F32)
        car = car + jnp.broadcast_to(total, car.shape)
    return (ws[0] if len(ws) == 1 else jnp.concatenate(ws, axis=1)), car


def _sb_prompt_kernel(bias_ref, q_ref, k_ref, v_ref, tri_ref, g_ref, o_ref, acc, car_a, car_b, *, bq):
    hp = pl.program_id(1)
    qi = pl.program_id(2)
    q2 = q_ref[...]
    lane = lax.broadcasted_iota(jnp.int32, (bq, LANES), 1)
    first = lane < HEAD_DIM
    zero = jnp.zeros_like(q2)

    def with_bias(qh, bias):
        b = jnp.full((bq, LANES), bias, F32)
        hi = b.astype(BF16).astype(F32)
        extra = jnp.where(lane == 0, hi, jnp.where(lane == 1, b - hi, 0.0))
        return jnp.concatenate([qh, extra.astype(BF16)], axis=1)

    qa = with_bias(jnp.where(first, q2, zero), bias_ref[2 * hp])
    qb = with_bias(jnp.where(first, zero, q2), bias_ref[2 * hp + 1])
    ones = jnp.where(lane < 2, 1.0, 0.0).astype(BF16)
    tri = tri_ref[...]
    for r in (acc, car_a, car_b):
        r[...] = jnp.zeros_like(r)

    def step(kb, mask):
        off = pl.multiple_of(kb * bq, bq)
        kblk = jnp.concatenate([k_ref[pl.ds(off, bq), :], ones], axis=1)
        vblk = v_ref[pl.ds(off, bq), :]
        vzero = jnp.zeros_like(vblk)
        wa, car_a[...] = _sb_weights(_nt(qa, kblk), tri, car_a[...], mask)
        wb, car_b[...] = _sb_weights(_nt(qb, kblk), tri, car_b[...], mask)
        acc[...] += _mm(wa, jnp.where(first, vblk, vzero)) + _mm(wb, jnp.where(first, vzero, vblk))

    rows = lax.broadcasted_iota(jnp.int32, (bq, bq), 0)
    cols = lax.broadcasted_iota(jnp.int32, (bq, bq), 1)
    step(qi, cols < rows)

    def body(i, c):
        step(qi - 1 - i, None)
        return c

    lax.fori_loop(0, qi, body, 0)

    o = acc[...]
    sq = o * o
    ms_a = jnp.sum(jnp.where(first, sq, 0.0), axis=-1, keepdims=True)
    ms_b = jnp.sum(jnp.where(first, 0.0, sq), axis=-1, keepdims=True)
    ms = jnp.where(first, ms_a, ms_b) * (1.0 / HEAD_DIM)
    o_ref[...] = (o * lax.rsqrt(ms + EPS) * g_ref[...]).astype(BF16)


def _tri_matrix(n):
    return (jnp.arange(n)[:, None] > jnp.arange(n)[None, :]).astype(BF16)


def _sb_prompt(q_bf, k_bf, v_bf, bias, g_row, *, batch, seq, bq, sub):
    t, sbw = q_bf.shape
    npair = sbw // LANES
    nq = seq // bq
    return pl.pallas_call(
        functools.partial(_sb_prompt_kernel, bq=bq),
        grid=(batch, npair, nq),
        in_specs=[pl.BlockSpec(memory_space=pltpu.SMEM),
                  pl.BlockSpec((bq, LANES), lambda b, p, i: (b * nq + i, p)),
                  pl.BlockSpec((seq, LANES), lambda b, p, i: (b, p)),
                  pl.BlockSpec((seq, LANES), lambda b, p, i: (b, p)),
                  pl.BlockSpec((sub, sub), lambda b, p, i: (0, 0)),
                  pl.BlockSpec((1, LANES), lambda b, p, i: (0, p))],
        out_specs=pl.BlockSpec((bq, LANES), lambda b, p, i: (b * nq + i, p)),
        out_shape=jax.ShapeDtypeStruct((t, sbw), BF16),
        scratch_shapes=[pltpu.VMEM((bq, LANES), F32)] * 3,
        compiler_params=_params("parallel", "parallel", "arbitrary"),
        name="sb_prompt",
    )(bias, q_bf, k_bf, v_bf, _tri_matrix(sub), g_row)


def _sb_sample_kernel(pt_ref, q_ref, kn_ref, vn_ref, bias_ref, tri_ref, bd_ref, g_ref, *rest,
                      pages, n_heads, n_new):
    k_refs = rest[:pages]
    v_refs = rest[pages:2 * pages]
    o_ref = rest[2 * pages]
    acc, car = rest[2 * pages + 1:]
    s = pl.program_id(1)
    q = q_ref[0]
    bias = bias_ref[...]
    tri = tri_ref[...]
    bk = tri.shape[0]
    nrow, width = q.shape

    @pl.when(s == 0)
    def _():
        pad = jnp.zeros((bk - kn_ref.shape[1], width), BF16)
        kblk = jnp.concatenate([kn_ref[0], pad], axis=0)
        vblk = jnp.concatenate([vn_ref[0], pad], axis=0)
        rows = lax.broadcasted_iota(jnp.int32, (nrow, bk), 0)
        cols = lax.broadcasted_iota(jnp.int32, (nrow, bk), 1)
        mask = cols < rows // n_heads
        w, car[...] = _sb_weights(_nt(q, kblk) + bias[:, :bk], tri, jnp.zeros(car.shape, F32), mask)
        acc[...] = _mm(w, vblk)

    kt = jnp.concatenate([k_refs[r][0].astype(BF16) for r in reversed(range(pages))], axis=1)
    vt = jnp.concatenate([v_refs[r][0].astype(BF16) for r in reversed(range(pages))], axis=1)
    w, car[...] = _sb_weights(_mm(q, kt) + bias, tri, car[...], None)
    acc[...] += _nt(w, vt)

    @pl.when(s == pl.num_programs(1) - 1)
    def _():
        rows = lax.broadcasted_iota(jnp.int32, (nrow, width), 0)
        cols = lax.broadcasted_iota(jnp.int32, (nrow, width), 1)
        own = jnp.where(cols // HEAD_DIM == rows % n_heads, acc[...], 0.0)
        o = jnp.sum(own.reshape(n_new, n_heads, width), axis=1)
        o = o * lax.rsqrt(_group_mean_sq(o, bd_ref[...]) + EPS) * g_ref[...]
        o_ref[0] = o.astype(BF16)


def _sb_sample(q_bf, kn_bf, vn_bf, bias, g_row, bd_sb, pool_k, pool_v, page_table, *, layer, n_new, pages):
    nb, npg = page_table.shape
    tok, n_heads = pool_k.shape[2:4]
    width = n_heads * HEAD_DIM
    nrow = n_new * n_heads
    nstep = npg // pages
    q3 = q_bf.reshape(nb, n_new, width)
    head_of_lane = jnp.arange(width) // HEAD_DIM
    qbd = jnp.where(head_of_lane[None, None, None, :] == jnp.arange(n_heads)[None, None, :, None],
                    q3[:, :, None, :], jnp.zeros((), BF16)).reshape(nb, nrow, width)
    pad_new = 8 - n_new
    kn = jnp.pad(kn_bf.reshape(nb, n_new, width), ((0, 0), (0, pad_new), (0, 0)))
    vn = jnp.pad(vn_bf.reshape(nb, n_new, width), ((0, 0), (0, pad_new), (0, 0)))
    bias_rows = jnp.broadcast_to(jnp.tile(bias, n_new)[:, None], (nrow, pages * tok)).astype(F32)

    def token_minor(pool):
        return jnp.transpose(pool, (0, 1, 3, 4, 2)).reshape(*pool.shape[:2], width, tok)

    def page_spec(r):
        return pl.BlockSpec((None, 1, width, tok),
                            lambda b, s, pt: (layer, pt[b, npg - 1 - (s * pages + r)], 0, 0))

    const = lambda shape: pl.BlockSpec(shape, lambda b, s, pt: (0,) * len(shape))
    per_b = lambda shape: pl.BlockSpec(shape, lambda b, s, pt: (b,) + (0,) * (len(shape) - 1))
    grid_spec = pltpu.PrefetchScalarGridSpec(
        num_scalar_prefetch=1,
        grid=(nb, nstep),
        in_specs=[per_b((1, nrow, width)), per_b((1, 8, width)), per_b((1, 8, width)),
                  const((nrow, pages * tok)), const((tok, tok)), const((width, width)), const((1, width))]
                 + [page_spec(r) for r in range(pages)] * 2,
        out_specs=per_b((1, n_new, width)),
        scratch_shapes=[pltpu.VMEM((nrow, width), F32), pltpu.VMEM((nrow, tok), F32)],
    )
    return pl.pallas_call(
        functools.partial(_sb_sample_kernel, pages=pages, n_heads=n_heads, n_new=n_new),
        grid_spec=grid_spec,
        out_shape=jax.ShapeDtypeStruct((nb, n_new, width), BF16),
        compiler_params=_params("parallel", "arbitrary"),
        name="sb_sample",
    )(page_table, qbd, kn, vn, bias_rows, _tri_matrix(tok), bd_sb, g_row,
      *([token_minor(pool_k)] * pages), *([token_minor(pool_v)] * pages))


def _conv_kernel(*refs, ts, taps, chunk, has_prev):
    if has_prev:
        hist_ref, prev_ref, cur_ref, w_ref, b_ref, lg_ref, lb_ref, o_ref, buf = refs
        head = jnp.where(pl.program_id(1) == 0, hist_ref[0], prev_ref[0])
    else:
        hist_ref, cur_ref, w_ref, b_ref, lg_ref, lb_ref, o_ref, buf = refs
        head = hist_ref[0]
    buf[0:CONV_PAD, :] = head
    buf[CONV_PAD:CONV_PAD + ts, :] = cur_ref[0]
    first = CONV_PAD - (taps - 1)
    for c0 in range(0, ts, chunk):
        acc = jnp.zeros((chunk, buf.shape[1]), F32)
        for k in range(taps):
            acc = acc + w_ref[k:k + 1, :] * buf[first + c0 + k:first + c0 + k + chunk, :]
        c = acc + b_ref[...]
        xc = c - jnp.mean(c, axis=-1, keepdims=True)
        y = xc * lax.rsqrt(jnp.mean(xc * xc, axis=-1, keepdims=True) + EPS) * lg_ref[...] + lb_ref[...]
        o_ref[0, c0:c0 + chunk, :] = (y * jax.nn.sigmoid(y)).astype(BF16)


def _conv_module(glu3, hist_pad, w, b, lg, lb, *, ts):
    nb, s, cch = glu3.shape
    taps = w.shape[0]
    has_prev = s > ts
    chunk = min(ts, 64)
    cur = pl.BlockSpec((1, ts, cch), lambda bb, i: (bb, i, 0))
    hist = pl.BlockSpec((1, CONV_PAD, cch), lambda bb, i: (bb, 0, 0))
    const = lambda a: pl.BlockSpec(a.shape, lambda bb, i: (0,) * a.ndim)
    in_specs, args = [hist], [hist_pad]
    if has_prev:
        per = ts // CONV_PAD
        in_specs.append(pl.BlockSpec((1, CONV_PAD, cch), lambda bb, i: (bb, jnp.maximum(i * per - 1, 0), 0)))
        args.append(glu3)
    in_specs += [cur, const(w), const(b), const(lg), const(lb)]
    args += [glu3, w, b, lg, lb]
    return pl.pallas_call(
        functools.partial(_conv_kernel, ts=ts, taps=taps, chunk=chunk, has_prev=has_prev),
        grid=(nb, s // ts),
        in_specs=in_specs,
        out_specs=cur,
        out_shape=jax.ShapeDtypeStruct((nb, s, cch), BF16),
        scratch_shapes=[pltpu.VMEM((CONV_PAD + max(ts, 8), cch), F32)],
        compiler_params=_params("parallel", "parallel"),
        name="conv_module",
    )(*args)


def _memkv_kernel(m_ref, g_ref, wk_ref, wv_ref, bd_ref, gk_ref, k_ref, v_ref):
    x = m_ref[...]
    xn = (x * lax.rsqrt(jnp.mean(x * x, axis=-1, keepdims=True) + EPS) * g_ref[...]).astype(BF16)
    k = _mm(xn, wk_ref[...])
    k_ref[...] = k * lax.rsqrt(_group_mean_sq(k, bd_ref[...]) + EPS) * gk_ref[...]
    v_ref[...] = _mm(xn, wv_ref[...])


def _memkv(mem, g, wk_bf, wv_bf, bd_mem, gk, *, tm):
    t, d = mem.shape
    memw = wk_bf.shape[1]
    full = lambda a: pl.BlockSpec(a.shape, lambda i: (0,) * a.ndim)
    out = pl.BlockSpec((tm, memw), lambda i: (i, 0))
    return pl.pallas_call(
        _memkv_kernel,
        grid=(t // tm,),
        in_specs=[pl.BlockSpec((tm, d), lambda i: (i, 0)), full(g), full(wk_bf), full(wv_bf), full(bd_mem), full(gk)],
        out_specs=[out, out],
        out_shape=[jax.ShapeDtypeStruct((t, memw), F32)] * 2,
        compiler_params=_params("parallel"),
        name="memkv",
    )(mem, g, wk_bf, wv_bf, bd_mem, gk)


def _mem_attn_kernel(q_ref, k_ref, v_ref, bd_ref, g_ref, o_ref, *, n_heads):
    q = q_ref[0]
    tq = q.shape[0]
    rows = max(tq, 8)
    if rows != tq:
        q = jnp.concatenate([q, jnp.zeros((rows - tq, q.shape[1]), q.dtype)], axis=0)
    kb = k_ref[0].astype(BF16)
    vb = v_ref[0].astype(BF16)
    head = lax.broadcasted_iota(jnp.int32, q.shape, 1) // HEAD_DIM
    zero = jnp.zeros_like(q)
    out = jnp.zeros(q.shape, F32)
    for h in range(n_heads):
        s = _nt(jnp.where(head == h, q, zero), kb) * (HEAD_DIM ** -0.5)
        p = jnp.exp(s - jnp.max(s, axis=-1, keepdims=True))
        oh = _mm(p.astype(BF16), vb) / jnp.sum(p, axis=-1, keepdims=True)
        out = jnp.where(head == h, oh, out)
    out = out * lax.rsqrt(_group_mean_sq(out, bd_ref[...]) + EPS) * g_ref[...]
    o_ref[0] = out[:tq].astype(BF16)


def _mem_attn(q3, k3, v3, bd_mem, g_row, *, tq):
    ng, s, memw = q3.shape
    m = k3.shape[1]
    const = lambda a: pl.BlockSpec(a.shape, lambda gq, i: (0,) * a.ndim)
    qspec = pl.BlockSpec((1, tq, memw), lambda gq, i: (gq, i, 0))
    kspec = pl.BlockSpec((1, m, memw), lambda gq, i: (gq, 0, 0))
    return pl.pallas_call(
        functools.partial(_mem_attn_kernel, n_heads=memw // HEAD_DIM),
        grid=(ng, s // tq),
        in_specs=[qspec, kspec, kspec, const(bd_mem), const(g_row)],
        out_specs=qspec,
        out_shape=jax.ShapeDtypeStruct((ng, s, memw), BF16),
        compiler_params=_params("parallel", "parallel"),
        name="mem_attn",
    )(q3, k3, v3, bd_mem, g_row)


def _outproj_kernel(x_ref, sb_ref, cv_ref, mm_ref, w_ref, o_ref):
    a = sb_ref.shape[1]
    b = a + cv_ref.shape[1]
    y = _mm(sb_ref[...], w_ref[0:a, :]) + _mm(cv_ref[...], w_ref[a:b, :]) + _mm(mm_ref[...], w_ref[b:, :])
    o_ref[...] = x_ref[...] + y


def _outproj(x, o_sb, o_conv, o_mem, w_bf, *, tm):
    t, d = x.shape
    row = lambda a: pl.BlockSpec((tm, a.shape[1]), lambda i: (i, 0))
    return pl.pallas_call(
        _outproj_kernel,
        grid=(t // tm,),
        in_specs=[row(x), row(o_sb), row(o_conv), row(o_mem), pl.BlockSpec(w_bf.shape, lambda i: (0, 0))],
        out_specs=row(x),
        out_shape=jax.ShapeDtypeStruct((t, d), F32),
        compiler_params=_params("parallel"),
        name="outproj",
    )(x, o_sb, o_conv, o_mem, w_bf)


def _top_values(s, n, with_rank=False):
    tops = []
    cur = s
    rank = jnp.full(s.shape, float(n), F32)
    for k in range(n):
        m = jnp.max(cur, axis=0, keepdims=True)
        tops.append(m)
        hit = cur >= jnp.where(m == NEG_INF, POS_INF, m)
        if with_rank:
            rank = jnp.where(hit, float(k), rank)
        cur = jnp.where(hit, NEG_INF, cur)
    return (tops, rank) if with_rank else tops


def _peer_select_tile(s1, s2):
    t1 = jnp.concatenate(_top_values(s1, TOPK), axis=0)
    tops2, rank2 = _top_values(s2, TOPK, with_rank=True)
    t2 = jnp.concatenate(tops2, axis=0)
    row8 = lax.broadcasted_iota(jnp.int32, (8, LANES), 0)
    parts = [t1[0:1] + t2]
    for a in range(1, 8):
        parts.append(jnp.where(row8 < TOPK // (a + 1), t1[a:a + 1] + t2[0:8], NEG_INF))
    parts.append(t1[8:TOPK] + t2[0:1])
    best = _top_values(jnp.concatenate(parts, axis=0), TOPK)
    tau = best[TOPK - 1]
    top = best[0]
    zsum = jnp.zeros_like(top)
    for bv in best:
        zsum = zsum + jnp.exp(bv - top)
    cnt = jnp.zeros(s1.shape, F32)
    for a in range(TOPK):
        sel = (t1[a:a + 1] + t2) >= tau
        cnt_a = jnp.sum(jnp.where(sel, 1.0, 0.0), axis=0, keepdims=True)
        cnt = jnp.where(s1 == t1[a:a + 1], cnt_a, cnt)
    e1 = jnp.exp(s1 - t1[0:1]) / zsum
    e2 = jnp.exp(s2 - t2[0:1])
    return cnt, e1, rank2, e2


def _twin_bf16(x):
    hi = lax.bitcast_convert_type(x.astype(BF16).astype(F32), jnp.uint32)
    return hi | (hi >> 16)


def _peer_select_kernel(x_ref, g_ref, wq_ref, keys_ref, xn_ref, cnt_ref, e1_ref, rk_ref, e2_ref,
                        qt_scr, s1_scr, s2_scr, *, n_heads):
    x = x_ref[...]
    xn = (x * lax.rsqrt(jnp.mean(x * x, axis=-1, keepdims=True) + EPS) * g_ref[...]).astype(BF16)
    xn_ref[...] = pltpu.bitcast(xn, jnp.uint32)
    qt_scr[...] = _nt(wq_ref[...], xn).astype(BF16)
    dh = keys_ref.shape[2]
    groups = x.shape[0] // LANES

    def head(h, c):
        r1 = pl.multiple_of(h * 2 * dh, 2 * dh)
        r2 = pl.multiple_of(h * 2 * dh + dh, dh)
        s1_scr[...] = _mm(keys_ref[2 * h], qt_scr[pl.ds(r1, dh), :])
        s2_scr[...] = _mm(keys_ref[2 * h + 1], qt_scr[pl.ds(r2, dh), :])
        for gi in range(groups):
            lanes = slice(gi * LANES, (gi + 1) * LANES)
            cnt, e1, rank2, e2 = _peer_select_tile(s1_scr[:, lanes], s2_scr[:, lanes])
            cnt_ref[h, gi] = _twin_bf16(cnt)
            e1_ref[h, gi] = _twin_bf16(e1)
            rk_ref[h, gi] = pltpu.bitcast(rank2.astype(BF16), jnp.uint32)
            e2_ref[h, gi] = pltpu.bitcast(e2.astype(BF16), jnp.uint32)
        return c

    lax.fori_loop(0, n_heads, head, 0)


def _peer_select(x, g, wq_t_bf, keys_bf, *, tm):
    t, d = x.shape
    n_heads = keys_bf.shape[0] // 2
    groups = tm // LANES
    full = lambda a: pl.BlockSpec(a.shape, lambda i: (0,) * a.ndim)
    sel = pl.BlockSpec((n_heads, groups, N_KEYS, LANES), lambda i: (0, i, 0, 0))
    row_shape = jax.ShapeDtypeStruct((n_heads, t // LANES, N_KEYS, LANES), jnp.uint32)
    col = pl.BlockSpec((n_heads, groups, N_KEYS // 2, LANES), lambda i: (0, i, 0, 0))
    col_shape = jax.ShapeDtypeStruct((n_heads, t // LANES, N_KEYS // 2, LANES), jnp.uint32)
    return pl.pallas_call(
        functools.partial(_peer_select_kernel, n_heads=n_heads),
        grid=(t // tm,),
        in_specs=[pl.BlockSpec((tm, d), lambda i: (i, 0)), full(g), full(wq_t_bf), full(keys_bf)],
        out_specs=[pl.BlockSpec((tm // 2, d), lambda i: (i, 0)), sel, sel, col, col],
        out_shape=[jax.ShapeDtypeStruct((t // 2, d), jnp.uint32), row_shape, row_shape, col_shape, col_shape],
        scratch_shapes=[pltpu.VMEM((wq_t_bf.shape[0], tm), BF16), pltpu.VMEM((N_KEYS, tm), F32),
                        pltpu.VMEM((N_KEYS, tm), F32)],
        compiler_params=_params("parallel"),
        name="peer_select",
    )(x, g, wq_t_bf, keys_bf)


def _peer_act_kernel(xn_ref, u_ref, o_ref):
    a = _nt(pltpu.bitcast(u_ref[...], BF16), pltpu.bitcast(xn_ref[...], BF16)).astype(BF16)
    o_ref[...] = pltpu.bitcast(0.5 * a * (1.0 + lax.erf(a * INV_SQRT2)), jnp.uint32)


def _peer_act(xn_pk, u_pk, *, tm, eb):
    d = u_pk.shape[1]
    t = 2 * xn_pk.shape[0]
    return pl.pallas_call(
        _peer_act_kernel,
        grid=(t // tm, 2 * u_pk.shape[0] // eb),
        in_specs=[pl.BlockSpec((tm // 2, d), lambda i, e: (i, 0)), pl.BlockSpec((eb // 2, d), lambda i, e: (e, 0))],
        out_specs=pl.BlockSpec((eb // 2, tm), lambda i, e: (e, i)),
        out_shape=jax.ShapeDtypeStruct((u_pk.shape[0], t), jnp.uint32),
        compiler_params=_params("parallel", "parallel"),
        name="peer_act",
    )(xn_pk, u_pk)


def _peer_mix_kernel(act_ref, vt_ref, cnt_ref, e1_ref, rk_ref, e2_ref, xmid_ref, o_ref,
                     wa_scr, acc_scr, *, splits, chunk):
    blk = pl.program_id(1)
    n_heads, groups = cnt_ref.shape[:2]
    eb = vt_ref.shape[1]
    rows_i = eb // N_KEYS

    @pl.when(blk == 0)
    def _():
        acc_scr[...] = jnp.zeros_like(acc_scr)

    def row_tile(ref, h, gi, i_row):
        words = jnp.broadcast_to(ref[h, gi, pl.ds(i_row, 1), :], (chunk // 2, LANES))
        return pltpu.bitcast(words, BF16)

    for s in range(splits):
        part = slice(s * eb // splits, (s + 1) * eb // splits)
        for r in range(s * rows_i // splits, (s + 1) * rows_i // splits):
            i_row = blk * rows_i + r
            for gi in range(groups):
                lanes = slice(gi * LANES, (gi + 1) * LANES)
                for j0 in range(0, N_KEYS, chunk):
                    cols = slice(j0 // 2, (j0 + chunk) // 2)
                    gate = jnp.zeros((chunk, LANES), BF16)
                    for h in range(n_heads):
                        rk = pltpu.bitcast(rk_ref[h, gi, cols, :], BF16)
                        e2 = pltpu.bitcast(e2_ref[h, gi, cols, :], BF16)
                        gate = gate + (jnp.where(rk < row_tile(cnt_ref, h, gi, i_row), e2, jnp.zeros_like(e2))
                                       * row_tile(e1_ref, h, gi, i_row))
                    e0 = r * N_KEYS + j0
                    act = pltpu.bitcast(act_ref[e0 // 2:(e0 + chunk) // 2, lanes], BF16)
                    wa_scr[e0:e0 + chunk, lanes] = gate * act
        acc_scr[...] += _mm(pltpu.bitcast(vt_ref[:, part], BF16), wa_scr[part, :])

    @pl.when(blk == pl.num_programs(1) - 1)
    def _():
        o_ref[...] = xmid_ref[...] + acc_scr[...].T


def _peer_mix(act_pk, vt_pk, cnt, e1, rk, e2, x_mid, *, tm, eb):
    t, d = x_mid.shape
    n_blocks = 2 * act_pk.shape[0] // eb
    n_heads = cnt.shape[0]
    groups = tm // LANES
    tok = pl.BlockSpec((tm, d), lambda i, e: (i, 0))
    sel = pl.BlockSpec((n_heads, groups, N_KEYS, LANES), lambda i, e: (0, i, 0, 0))
    col = pl.BlockSpec((n_heads, groups, N_KEYS // 2, LANES), lambda i, e: (0, i, 0, 0))
    return pl.pallas_call(
        functools.partial(_peer_mix_kernel, splits=2, chunk=N_KEYS),
        grid=(t // tm, n_blocks),
        in_specs=[pl.BlockSpec((eb // 2, tm), lambda i, e: (e, i)),
                  pl.BlockSpec((d // 2, eb), lambda i, e: (0, e)),
                  sel, sel, col, col, tok],
        out_specs=tok,
        out_shape=jax.ShapeDtypeStruct((t, d), F32),
        scratch_shapes=[pltpu.VMEM((eb, tm), BF16), pltpu.VMEM((d, tm), F32)],
        compiler_params=_params("parallel", "arbitrary"),
        name="peer_mix",
    )(act_pk, vt_pk, cnt, e1, rk, e2, x_mid)


def _pack_kernel(x_ref, o_ref):
    o_ref[...] = pltpu.bitcast(x_ref[...].astype(BF16), jnp.uint32)


def _pack_bf16(x, *, tr):
    r, c = x.shape
    return pl.pallas_call(
        _pack_kernel,
        grid=(r // tr,),
        in_specs=[pl.BlockSpec((tr, c), lambda i: (i, 0))],
        out_specs=pl.BlockSpec((tr // 2, c), lambda i: (i, 0)),
        out_shape=jax.ShapeDtypeStruct((r // 2, c), jnp.uint32),
        compiler_params=_params("parallel"),
        name="pack_bf16",
    )(x)


def _block_diag(width):
    idx = jnp.arange(width) // HEAD_DIM
    return jnp.where(idx[:, None] == idx[None, :], 1.0 / HEAD_DIM, 0.0).astype(BF16)


def _tile(n, want):
    return want if n % want == 0 else n


def kernel(x_prompt, x_sample, mem_prompt, cache_sb_k, cache_sb_v, page_table, cache_mem_k, cache_mem_v,
           state_conv, g_mix, w_in, sb_bias, sb_out_g, conv_w, conv_b, conv_ln_g, conv_ln_b, g_mem, w_mem_k,
           w_mem_v, mem_q_g, mem_k_g, mem_out_g, w_out, g_ffn, peer_w_q, peer_sub_keys, peer_u, peer_v):
    depth = w_in.shape[0]
    batch, seq, d = x_prompt.shape
    nb, n_new, _ = x_sample.shape
    sb_heads = sb_bias.shape[1]
    sbw = sb_heads * HEAD_DIM
    cch = conv_w.shape[2]
    taps = conv_w.shape[1]
    mem_heads = mem_out_g.shape[1]
    memw = mem_heads * HEAD_DIM
    mem_tok = mem_prompt.shape[1]
    n_pool, page = cache_sb_k.shape[1:3]
    n_exp = peer_u.shape[1]

    bd_mem = _block_diag(memw)
    bd_sb = _block_diag(sbw)
    xp = x_prompt.reshape(batch * seq, d)
    xs = x_sample.reshape(nb * n_new, d)
    tp = _tile(batch * seq, 512)
    tsm = _tile(nb * n_new, 512)
    outs = [[] for _ in range(8)]

    for l in range(depth):
        row = lambda a: a[l].reshape(1, -1)
        w_in_bf = w_in[l].astype(BF16)
        bias2 = sb_bias[l] * LOG2E
        w_out_bf = w_out[l].astype(BF16)
        gq = jnp.tile(mem_q_g[l], mem_heads).reshape(1, memw)
        gk = jnp.tile(mem_k_g[l], mem_heads).reshape(1, memw)
        conv_args = (conv_w[l], row(conv_b), row(conv_ln_g), row(conv_ln_b))

        mk, mv = _memkv(mem_prompt.reshape(batch * mem_tok, d), row(g_mem), w_mem_k[l].astype(BF16),
                        w_mem_v[l].astype(BF16), bd_mem, gk, tm=_tile(batch * mem_tok, 512))

        q, k, v, kb, vb, glu, qm = _inproj(xp, row(g_mix), w_in_bf, bd_mem, gq, sbw=sbw, cch=cch, memw=memw, tm=tp)
        o_sb = _sb_prompt(q, kb, vb, bias2, row(sb_out_g), batch=batch, seq=seq,
                          bq=_tile(seq, 512), sub=256)
        glu3 = glu.reshape(batch, seq, cch)
        o_conv = _conv_module(glu3, jnp.zeros((batch, CONV_PAD, cch), F32), *conv_args, ts=_tile(seq, 256))
        o_mem = _mem_attn(qm.reshape(batch, seq, memw), mk.reshape(batch, mem_tok, memw),
                          mv.reshape(batch, mem_tok, memw), bd_mem, row(mem_out_g), tq=_tile(seq, 512))
        xp_mid = _outproj(xp, o_sb, o_conv.reshape(-1, cch), o_mem.reshape(-1, memw), w_out_bf, tm=tp)

        q_s, k_s, v_s, kb_s, vb_s, glu_s, qm_s = _inproj(xs, row(g_mix), w_in_bf, bd_mem, gq,
                                                        sbw=sbw, cch=cch, memw=memw, tm=tsm)
        npg = page_table.shape[1]
        o_sb_s = _sb_sample(q_s, kb_s, vb_s, bias2, row(sb_out_g), bd_sb,
                            cache_sb_k, cache_sb_v, page_table, layer=l, n_new=n_new,
                            pages=8 if npg % 8 == 0 else npg)
        glu_s3 = glu_s.reshape(nb, n_new, cch)
        hist_s = jnp.pad(state_conv[l], ((0, 0), (CONV_PAD - (taps - 1), 0), (0, 0)))
        o_conv_s = _conv_module(glu_s3, hist_s, *conv_args, ts=n_new)
        o_mem_s = _mem_attn(qm_s.reshape(nb, n_new, memw), cache_mem_k[l].reshape(nb, mem_tok, memw),
                            cache_mem_v[l].reshape(nb, mem_tok, memw), bd_mem, row(mem_out_g), tq=n_new)
        xs_mid = _outproj(xs, o_sb_s.reshape(-1, sbw), o_conv_s.reshape(-1, cch), o_mem_s.reshape(-1, memw),
                          w_out_bf, tm=tsm)

        wq_t = peer_w_q[l].T.astype(BF16)
        keys = peer_sub_keys[l].reshape(-1, N_KEYS, peer_sub_keys.shape[-1]).astype(BF16)
        u_pk = _pack_bf16(peer_u[l], tr=_tile(n_exp, 1024))
        vt_pk = _pack_bf16(peer_v[l].T, tr=64)
        new = []
        for x_mid, tm in ((xp_mid, tp), (xs_mid, tsm)):
            xn_pk, cnt, e1, rk, e2 = _peer_select(x_mid, row(g_ffn), wq_t, keys, tm=tm)
            act_pk = _peer_act(xn_pk, u_pk, tm=tm, eb=_tile(n_exp, 1024))
            new.append(_peer_mix(act_pk, vt_pk, cnt, e1, rk, e2, x_mid, tm=tm, eb=_tile(n_exp, 1024)))
        xp, xs = new

        hist_rows = taps - 1
        cs = jnp.concatenate([state_conv[l], glu_s3], axis=1)[:, -hist_rows:]
        for lst, val in zip(outs, (k.reshape(batch, seq, sb_heads, HEAD_DIM), v.reshape(batch, seq, sb_heads, HEAD_DIM),
                                   k_s.reshape(nb, n_new, sb_heads, HEAD_DIM), v_s.reshape(nb, n_new, sb_heads, HEAD_DIM),
                                   mk.reshape(batch, mem_tok, mem_heads, HEAD_DIM),
                                   mv.reshape(batch, mem_tok, mem_heads, HEAD_DIM),
                                   glu3[:, seq - hist_rows:], cs)):
            lst.append(val)

    return (xp.reshape(batch, seq, d), xs.reshape(nb, n_new, d), *[jnp.stack(o) for o in outs])
```

```python
import functools
import math

import jax
import jax.numpy as jnp
from jax import lax
from jax.experimental import pallas as pl
from jax.experimental.pallas import tpu as pltpu

F32 = jnp.float32
BF16 = jnp.bfloat16

EPS = 1e-6
HEAD_DIM = 64
LANES = 128
TOPK = 16
N_KEYS = 128
CONV_PAD = 32
VMEM_LIMIT = 56 * 1024 * 1024
NEG_INF = float("-inf")
POS_INF = float("inf")
INV_SQRT2 = 1.0 / math.sqrt(2.0)
LOG2E = 1.0 / math.log(2.0)


def _nt(a, b):
    return lax.dot_general(a, b, (((1,), (1,)), ((), ())), preferred_element_type=F32)


def _mm(a, b):
    return jnp.dot(a, b, preferred_element_type=F32)


def _group_mean_sq(x, bd):
    sq = x * x
    hi = sq.astype(BF16)
    lo = (sq - hi.astype(F32)).astype(BF16)
    return _mm(hi, bd) + _mm(lo, bd)


def _params(*sem):
    return pltpu.CompilerParams(dimension_semantics=sem, vmem_limit_bytes=VMEM_LIMIT)


def _inproj_kernel(x_ref, g_ref, w_ref, bd_ref, gq_ref,
                   q_ref, k_ref, v_ref, kb_ref, vb_ref, glu_ref, qm_ref, *, sbw, cch):
    x = x_ref[...]
    xn = x * lax.rsqrt(jnp.mean(x * x, axis=-1, keepdims=True) + EPS) * g_ref[...]
    z = _mm(xn.astype(BF16), w_ref[...])
    q_ref[...] = (z[:, :sbw] * (LOG2E * HEAD_DIM ** -0.5)).astype(BF16)
    k = z[:, sbw:2 * sbw]
    v = z[:, 2 * sbw:3 * sbw]
    k_ref[...] = k
    v_ref[...] = v
    kb_ref[...] = k.astype(BF16)
    vb_ref[...] = v.astype(BF16)
    o = 3 * sbw
    glu_ref[...] = z[:, o:o + cch] * jax.nn.sigmoid(z[:, o + cch:o + 2 * cch])
    qm = z[:, o + 2 * cch:]
    qm_ref[...] = (qm * lax.rsqrt(_group_mean_sq(qm, bd_ref[...]) + EPS) * gq_ref[...]).astype(BF16)


def _inproj(x, g, w_bf, bd_mem, gq, *, sbw, cch, memw, tm):
    t, d = x.shape
    cols = w_bf.shape[1]
    row = lambda width: pl.BlockSpec((tm, width), lambda i: (i, 0))
    full = lambda a: pl.BlockSpec(a.shape, lambda i: (0,) * a.ndim)
    return pl.pallas_call(
        functools.partial(_inproj_kernel, sbw=sbw, cch=cch),
        grid=(t // tm,),
        in_specs=[row(d), full(g), full(w_bf), full(bd_mem), full(gq)],
        out_specs=[row(sbw), row(sbw), row(sbw), row(sbw), row(sbw), row(cch), row(memw)],
        out_shape=[jax.ShapeDtypeStruct((t, sbw), BF16), jax.ShapeDtypeStruct((t, sbw), F32),
                   jax.ShapeDtypeStruct((t, sbw), F32), jax.ShapeDtypeStruct((t, sbw), BF16),
                   jax.ShapeDtypeStruct((t, sbw), BF16), jax.ShapeDtypeStruct((t, cch), F32),
                   jax.ShapeDtypeStruct((t, memw), BF16)],
        compiler_params=_params("parallel"),
        name="inproj",
    )(x, g, w_bf, bd_mem, gq)


def _sb_weights(z, tri, car, mask):
    ks, sub = z.shape[1], tri.shape[0]
    neg_abs = lax.bitcast_convert_type(lax.bitcast_convert_type(z, jnp.uint32) | jnp.uint32(0x80000000), F32)
    lb = jnp.minimum(z, 0.0) - jnp.log2(1.0 + jnp.exp2(neg_abs))
    l1m = lb - z
    if mask is not None:
        l1m = jnp.where(mask, l1m, 0.0)
    ws = [None] * (ks // sub)
    for j in reversed(range(ks // sub)):
        cols = slice(j * sub, (j + 1) * sub)
        lj = l1m[:, cols].astype(BF16)
        tail = _mm(lj, tri)
        w = jnp.exp2(lb[:, cols] + tail + jnp.tile(car, (1, sub // car.shape[1])))
        if mask is not None:
            w = jnp.where(mask[:, cols], w, 0.0)
        ws[j] = w.astype(BF16)
        total = tail[:, 0:1] + lj[:, 0:1].astype(F32)
        car = car + jnp.broadcast_to(total, car.shape)
    return (ws[0] if len(ws) == 1 else jnp.concatenate(ws, axis=1)), car


def _sb_prompt_kernel(bias_ref, q_ref, k_ref, v_ref, tri_ref, g_ref, o_ref, acc, car_a, car_b, *, bq):
    hp = pl.program_id(1)
    qi = pl.program_id(2)
    q2 = q_ref[...]
    lane = lax.broadcasted_iota(jnp.int32, (bq, LANES), 1)
    first = lane < HEAD_DIM
    zero = jnp.zeros_like(q2)

    def with_bias(qh, bias):
        b = jnp.full((bq, LANES), bias, F32)
        hi = b.astype(BF16).astype(F32)
        extra = jnp.where(lane == 0, hi, jnp.where(lane == 1, b - hi, 0.0))
        return jnp.concatenate([qh, extra.astype(BF16)], axis=1)

    qa = with_bias(jnp.where(first, q2, zero), bias_ref[2 * hp])
    qb = with_bias(jnp.where(first, zero, q2), bias_ref[2 * hp + 1])
    ones = jnp.where(lane < 2, 1.0, 0.0).astype(BF16)
    tri = tri_ref[...]
    for r in (acc, car_a, car_b):
        r[...] = jnp.zeros_like(r)

    def step(kb, mask):
        off = pl.multiple_of(kb * bq, bq)
        kblk = jnp.concatenate([k_ref[pl.ds(off, bq), :], ones], axis=1)
        vblk = v_ref[pl.ds(off, bq), :]
        vzero = jnp.zeros_like(vblk)
        wa, car_a[...] = _sb_weights(_nt(qa, kblk), tri, car_a[...], mask)
        wb, car_b[...] = _sb_weights(_nt(qb, kblk), tri, car_b[...], mask)
        acc[...] += _mm(wa, jnp.where(first, vblk, vzero)) + _mm(wb, jnp.where(first, vzero, vblk))

    rows = lax.broadcasted_iota(jnp.int32, (bq, bq), 0)
    cols = lax.broadcasted_iota(jnp.int32, (bq, bq), 1)
    step(qi, cols < rows)

    def body(i, c):
        step(qi - 1 - i, None)
        return c

    lax.fori_loop(0, qi, body, 0)

    o = acc[...]
    sq = o * o
    ms_a = jnp.sum(jnp.where(first, sq, 0.0), axis=-1, keepdims=True)
    ms_b = jnp.sum(jnp.where(first, 0.0, sq), axis=-1, keepdims=True)
    ms = jnp.where(first, ms_a, ms_b) * (1.0 / HEAD_DIM)
    o_ref[...] = (o * lax.rsqrt(ms + EPS) * g_ref[...]).astype(BF16)


def _tri_matrix(n):
    return (jnp.arange(n)[:, None] > jnp.arange(n)[None, :]).astype(BF16)


def _sb_prompt(q_bf, k_bf, v_bf, bias, g_row, *, batch, seq, bq, sub):
    t, sbw = q_bf.shape
    npair = sbw // LANES
    nq = seq // bq
    return pl.pallas_call(
        functools.partial(_sb_prompt_kernel, bq=bq),
        grid=(batch, npair, nq),
        in_specs=[pl.BlockSpec(memory_space=pltpu.SMEM),
                  pl.BlockSpec((bq, LANES), lambda b, p, i: (b * nq + i, p)),
                  pl.BlockSpec((seq, LANES), lambda b, p, i: (b, p)),
                  pl.BlockSpec((seq, LANES), lambda b, p, i: (b, p)),
                  pl.BlockSpec((sub, sub), lambda b, p, i: (0, 0)),
                  pl.BlockSpec((1, LANES), lambda b, p, i: (0, p))],
        out_specs=pl.BlockSpec((bq, LANES), lambda b, p, i: (b * nq + i, p)),
        out_shape=jax.ShapeDtypeStruct((t, sbw), BF16),
        scratch_shapes=[pltpu.VMEM((bq, LANES), F32)] * 3,
        compiler_params=_params("parallel", "parallel", "arbitrary"),
        name="sb_prompt",
    )(bias, q_bf, k_bf, v_bf, _tri_matrix(sub), g_row)


def _sb_sample_kernel(pt_ref, q_ref, kn_ref, vn_ref, bias_ref, tri_ref, bd_ref, g_ref, *rest,
                      pages, n_heads, n_new):
    k_refs = rest[:pages]
    v_refs = rest[pages:2 * pages]
    o_ref = rest[2 * pages]
    acc, car = rest[2 * pages + 1:]
    s = pl.program_id(1)
    q = q_ref[0]
    bias = bias_ref[...]
    tri = tri_ref[...]
    bk = tri.shape[0]
    nrow, width = q.shape

    @pl.when(s == 0)
    def _():
        pad = jnp.zeros((bk - kn_ref.shape[1], width), BF16)
        kblk = jnp.concatenate([kn_ref[0], pad], axis=0)
        vblk = jnp.concatenate([vn_ref[0], pad], axis=0)
        rows = lax.broadcasted_iota(jnp.int32, (nrow, bk), 0)
        cols = lax.broadcasted_iota(jnp.int32, (nrow, bk), 1)
        mask = cols < rows // n_heads
        w, car[...] = _sb_weights(_nt(q, kblk) + bias[:, :bk], tri, jnp.zeros(car.shape, F32), mask)
        acc[...] = _mm(w, vblk)

    kt = jnp.concatenate([k_refs[r][0].astype(BF16) for r in reversed(range(pages))], axis=1)
    vt = jnp.concatenate([v_refs[r][0].astype(BF16) for r in reversed(range(pages))], axis=1)
    w, car[...] = _sb_weights(_mm(q, kt) + bias, tri, car[...], None)
    acc[...] += _nt(w, vt)

    @pl.when(s == pl.num_programs(1) - 1)
    def _():
        rows = lax.broadcasted_iota(jnp.int32, (nrow, width), 0)
        cols = lax.broadcasted_iota(jnp.int32, (nrow, width), 1)
        own = jnp.where(cols // HEAD_DIM == rows % n_heads, acc[...], 0.0)
        o = jnp.sum(own.reshape(n_new, n_heads, width), axis=1)
        o = o * lax.rsqrt(_group_mean_sq(o, bd_ref[...]) + EPS) * g_ref[...]
        o_ref[0] = o.astype(BF16)


def _sb_sample(q_bf, kn_bf, vn_bf, bias, g_row, bd_sb, pool_k, pool_v, page_table, *, layer, n_new, pages):
    nb, npg = page_table.shape
    tok, n_heads = pool_k.shape[2:4]
    width = n_heads * HEAD_DIM
    nrow = n_new * n_heads
    nstep = npg // pages
    q3 = q_bf.reshape(nb, n_new, width)
    head_of_lane = jnp.arange(width) // HEAD_DIM
    qbd = jnp.where(head_of_lane[None, None, None, :] == jnp.arange(n_heads)[None, None, :, None],
                    q3[:, :, None, :], jnp.zeros((), BF16)).reshape(nb, nrow, width)
    pad_new = 8 - n_new
    kn = jnp.pad(kn_bf.reshape(nb, n_new, width), ((0, 0), (0, pad_new), (0, 0)))
    vn = jnp.pad(vn_bf.reshape(nb, n_new, width), ((0, 0), (0, pad_new), (0, 0)))
    bias_rows = jnp.broadcast_to(jnp.tile(bias, n_new)[:, None], (nrow, pages * tok)).astype(F32)

    def token_minor(pool):
        return jnp.transpose(pool, (0, 1, 3, 4, 2)).reshape(*pool.shape[:2], width, tok)

    def page_spec(r):
        return pl.BlockSpec((None, 1, width, tok),
                            lambda b, s, pt: (layer, pt[b, npg - 1 - (s * pages + r)], 0, 0))

    const = lambda shape: pl.BlockSpec(shape, lambda b, s, pt: (0,) * len(shape))
    per_b = lambda shape: pl.BlockSpec(shape, lambda b, s, pt: (b,) + (0,) * (len(shape) - 1))
    grid_spec = pltpu.PrefetchScalarGridSpec(
        num_scalar_prefetch=1,
        grid=(nb, nstep),
        in_specs=[per_b((1, nrow, width)), per_b((1, 8, width)), per_b((1, 8, width)),
                  const((nrow, pages * tok)), const((tok, tok)), const((width, width)), const((1, width))]
                 + [page_spec(r) for r in range(pages)] * 2,
        out_specs=per_b((1, n_new, width)),
        scratch_shapes=[pltpu.VMEM((nrow, width), F32), pltpu.VMEM((nrow, tok), F32)],
    )
    return pl.pallas_call(
        functools.partial(_sb_sample_kernel, pages=pages, n_heads=n_heads, n_new=n_new),
        grid_spec=grid_spec,
        out_shape=jax.ShapeDtypeStruct((nb, n_new, width), BF16),
        compiler_params=_params("parallel", "arbitrary"),
        name="sb_sample",
    )(page_table, qbd, kn, vn, bias_rows, _tri_matrix(tok), bd_sb, g_row,
      *([token_minor(pool_k)] * pages), *([token_minor(pool_v)] * pages))


def _conv_kernel(*refs, ts, taps, chunk, has_prev):
    if has_prev:
        hist_ref, prev_ref, cur_ref, w_ref, b_ref, lg_ref, lb_ref, o_ref, buf = refs
        head = jnp.where(pl.program_id(1) == 0, hist_ref[0], prev_ref[0])
    else:
        hist_ref, cur_ref, w_ref, b_ref, lg_ref, lb_ref, o_ref, buf = refs
        head = hist_ref[0]
    buf[0:CONV_PAD, :] = head
    buf[CONV_PAD:CONV_PAD + ts, :] = cur_ref[0]
    first = CONV_PAD - (taps - 1)
    for c0 in range(0, ts, chunk):
        acc = jnp.zeros((chunk, buf.shape[1]), F32)
        for k in range(taps):
            acc = acc + w_ref[k:k + 1, :] * buf[first + c0 + k:first + c0 + k + chunk, :]
        c = acc + b_ref[...]
        xc = c - jnp.mean(c, axis=-1, keepdims=True)
        y = xc * lax.rsqrt(jnp.mean(xc * xc, axis=-1, keepdims=True) + EPS) * lg_ref[...] + lb_ref[...]
        o_ref[0, c0:c0 + chunk, :] = (y * jax.nn.sigmoid(y)).astype(BF16)


def _conv_module(glu3, hist_pad, w, b, lg, lb, *, ts):
    nb, s, cch = glu3.shape
    taps = w.shape[0]
    has_prev = s > ts
    chunk = min(ts, 64)
    cur = pl.BlockSpec((1, ts, cch), lambda bb, i: (bb, i, 0))
    hist = pl.BlockSpec((1, CONV_PAD, cch), lambda bb, i: (bb, 0, 0))
    const = lambda a: pl.BlockSpec(a.shape, lambda bb, i: (0,) * a.ndim)
    in_specs, args = [hist], [hist_pad]
    if has_prev:
        per = ts // CONV_PAD
        in_specs.append(pl.BlockSpec((1, CONV_PAD, cch), lambda bb, i: (bb, jnp.maximum(i * per - 1, 0), 0)))
        args.append(glu3)
    in_specs += [cur, const(w), const(b), const(lg), const(lb)]
    args += [glu3, w, b, lg, lb]
    return pl.pallas_call(
        functools.partial(_conv_kernel, ts=ts, taps=taps, chunk=chunk, has_prev=has_prev),
        grid=(nb, s // ts),
        in_specs=in_specs,
        out_specs=cur,
        out_shape=jax.ShapeDtypeStruct((nb, s, cch), BF16),
        scratch_shapes=[pltpu.VMEM((CONV_PAD + max(ts, 8), cch), F32)],
        compiler_params=_params("parallel", "parallel"),
        name="conv_module",
    )(*args)


def _memkv_kernel(m_ref, g_ref, wk_ref, wv_ref, bd_ref, gk_ref, k_ref, v_ref):
    x = m_ref[...]
    xn = (x * lax.rsqrt(jnp.mean(x * x, axis=-1, keepdims=True) + EPS) * g_ref[...]).astype(BF16)
    k = _mm(xn, wk_ref[...])
    k_ref[...] = k * lax.rsqrt(_group_mean_sq(k, bd_ref[...]) + EPS) * gk_ref[...]
    v_ref[...] = _mm(xn, wv_ref[...])


def _memkv(mem, g, wk_bf, wv_bf, bd_mem, gk, *, tm):
    t, d = mem.shape
    memw = wk_bf.shape[1]
    full = lambda a: pl.BlockSpec(a.shape, lambda i: (0,) * a.ndim)
    out = pl.BlockSpec((tm, memw), lambda i: (i, 0))
    return pl.pallas_call(
        _memkv_kernel,
        grid=(t // tm,),
        in_specs=[pl.BlockSpec((tm, d), lambda i: (i, 0)), full(g), full(wk_bf), full(wv_bf), full(bd_mem), full(gk)],
        out_specs=[out, out],
        out_shape=[jax.ShapeDtypeStruct((t, memw), F32)] * 2,
        compiler_params=_params("parallel"),
        name="memkv",
    )(mem, g, wk_bf, wv_bf, bd_mem, gk)


def _mem_attn_kernel(q_ref, k_ref, v_ref, bd_ref, g_ref, o_ref, *, n_heads):
    q = q_ref[0]
    tq = q.shape[0]
    rows = max(tq, 8)
    if rows != tq:
        q = jnp.concatenate([q, jnp.zeros((rows - tq, q.shape[1]), q.dtype)], axis=0)
    kb = k_ref[0].astype(BF16)
    vb = v_ref[0].astype(BF16)
    head = lax.broadcasted_iota(jnp.int32, q.shape, 1) // HEAD_DIM
    zero = jnp.zeros_like(q)
    out = jnp.zeros(q.shape, F32)
    for h in range(n_heads):
        s = _nt(jnp.where(head == h, q, zero), kb) * (HEAD_DIM ** -0.5)
        p = jnp.exp(s - jnp.max(s, axis=-1, keepdims=True))
        oh = _mm(p.astype(BF16), vb) / jnp.sum(p, axis=-1, keepdims=True)
        out = jnp.where(head == h, oh, out)
    out = out * lax.rsqrt(_group_mean_sq(out, bd_ref[...]) + EPS) * g_ref[...]
    o_ref[0] = out[:tq].astype(BF16)


def _mem_attn(q3, k3, v3, bd_mem, g_row, *, tq):
    ng, s, memw = q3.shape
    m = k3.shape[1]
    const = lambda a: pl.BlockSpec(a.shape, lambda gq, i: (0,) * a.ndim)
    qspec = pl.BlockSpec((1, tq, memw), lambda gq, i: (gq, i, 0))
    kspec = pl.BlockSpec((1, m, memw), lambda gq, i: (gq, 0, 0))
    return pl.pallas_call(
        functools.partial(_mem_attn_kernel, n_heads=memw // HEAD_DIM),
        grid=(ng, s // tq),
        in_specs=[qspec, kspec, kspec, const(bd_mem), const(g_row)],
        out_specs=qspec,
        out_shape=jax.ShapeDtypeStruct((ng, s, memw), BF16),
        compiler_params=_params("parallel", "parallel"),
        name="mem_attn",
    )(q3, k3, v3, bd_mem, g_row)


def _outproj_kernel(x_ref, sb_ref, cv_ref, mm_ref, w_ref, o_ref):
    a = sb_ref.shape[1]
    b = a + cv_ref.shape[1]
    y = _mm(sb_ref[...], w_ref[0:a, :]) + _mm(cv_ref[...], w_ref[a:b, :]) + _mm(mm_ref[...], w_ref[b:, :])
    o_ref[...] = x_ref[...] + y


def _outproj(x, o_sb, o_conv, o_mem, w_bf, *, tm):
    t, d = x.shape
    row = lambda a: pl.BlockSpec((tm, a.shape[1]), lambda i: (i, 0))
    return pl.pallas_call(
        _outproj_kernel,
        grid=(t // tm,),
        in_specs=[row(x), row(o_sb), row(o_conv), row(o_mem), pl.BlockSpec(w_bf.shape, lambda i: (0, 0))],
        out_specs=row(x),
        out_shape=jax.ShapeDtypeStruct((t, d), F32),
        compiler_params=_params("parallel"),
        name="outproj",
    )(x, o_sb, o_conv, o_mem, w_bf)


def _top_values(s, n, with_rank=False):
    tops = []
    cur = s
    rank = jnp.full(s.shape, float(n), F32)
    for k in range(n):
        m = jnp.max(cur, axis=0, keepdims=True)
        tops.append(m)
        hit = cur >= jnp.where(m == NEG_INF, POS_INF, m)
        if with_rank:
            rank = jnp.where(hit, float(k), rank)
        cur = jnp.where(hit, NEG_INF, cur)
    return (tops, rank) if with_rank else tops


def _peer_select_tile(s1, s2):
    t1 = jnp.concatenate(_top_values(s1, TOPK), axis=0)
    tops2, rank2 = _top_values(s2, TOPK, with_rank=True)
    t2 = jnp.concatenate(tops2, axis=0)
    row8 = lax.broadcasted_iota(jnp.int32, (8, LANES), 0)
    parts = [t1[0:1] + t2]
    for a in range(1, 8):
        parts.append(jnp.where(row8 < TOPK // (a + 1), t1[a:a + 1] + t2[0:8], NEG_INF))
    parts.append(t1[8:TOPK] + t2[0:1])
    best = _top_values(jnp.concatenate(parts, axis=0), TOPK)
    tau = best[TOPK - 1]
    top = best[0]
    zsum = jnp.zeros_like(top)
    for bv in best:
        zsum = zsum + jnp.exp(bv - top)
    cnt = jnp.zeros(s1.shape, F32)
    for a in range(TOPK):
        sel = (t1[a:a + 1] + t2) >= tau
        cnt_a = jnp.sum(jnp.where(sel, 1.0, 0.0), axis=0, keepdims=True)
        cnt = jnp.where(s1 == t1[a:a + 1], cnt_a, cnt)
    e1 = jnp.exp(s1 - t1[0:1]) / zsum
    e2 = jnp.exp(s2 - t2[0:1])
    return cnt, e1, rank2, e2


def _twin_bf16(x):
    hi = lax.bitcast_convert_type(x.astype(BF16).astype(F32), jnp.uint32)
    return hi | (hi >> 16)


def _peer_select_kernel(x_ref, g_ref, wq_ref, keys_ref, xn_ref, cnt_ref, e1_ref, rk_ref, e2_ref,
                        qt_scr, s1_scr, s2_scr, *, n_heads):
    x = x_ref[...]
    xn = (x * lax.rsqrt(jnp.mean(x * x, axis=-1, keepdims=True) + EPS) * g_ref[...]).astype(BF16)
    xn_ref[...] = pltpu.bitcast(xn, jnp.uint32)
    qt_scr[...] = _nt(wq_ref[...], xn).astype(BF16)
    dh = keys_ref.shape[2]
    groups = x.shape[0] // LANES

    def head(h, c):
        r1 = pl.multiple_of(h * 2 * dh, 2 * dh)
        r2 = pl.multiple_of(h * 2 * dh + dh, dh)
        s1_scr[...] = _mm(keys_ref[2 * h], qt_scr[pl.ds(r1, dh), :])
        s2_scr[...] = _mm(keys_ref[2 * h + 1], qt_scr[pl.ds(r2, dh), :])
        for gi in range(groups):
            lanes = slice(gi * LANES, (gi + 1) * LANES)
            cnt, e1, rank2, e2 = _peer_select_tile(s1_scr[:, lanes], s2_scr[:, lanes])
            cnt_ref[h, gi] = _twin_bf16(cnt)
            e1_ref[h, gi] = _twin_bf16(e1)
            rk_ref[h, gi] = pltpu.bitcast(rank2.astype(BF16), jnp.uint32)
            e2_ref[h, gi] = pltpu.bitcast(e2.astype(BF16), jnp.uint32)
        return c

    lax.fori_loop(0, n_heads, head, 0)


def _peer_select(x, g, wq_t_bf, keys_bf, *, tm):
    t, d = x.shape
    n_heads = keys_bf.shape[0] // 2
    groups = tm // LANES
    full = lambda a: pl.BlockSpec(a.shape, lambda i: (0,) * a.ndim)
    sel = pl.BlockSpec((n_heads, groups, N_KEYS, LANES), lambda i: (0, i, 0, 0))
    row_shape = jax.ShapeDtypeStruct((n_heads, t // LANES, N_KEYS, LANES), jnp.uint32)
    col = pl.BlockSpec((n_heads, groups, N_KEYS // 2, LANES), lambda i: (0, i, 0, 0))
    col_shape = jax.ShapeDtypeStruct((n_heads, t // LANES, N_KEYS // 2, LANES), jnp.uint32)
    return pl.pallas_call(
        functools.partial(_peer_select_kernel, n_heads=n_heads),
        grid=(t // tm,),
        in_specs=[pl.BlockSpec((tm, d), lambda i: (i, 0)), full(g), full(wq_t_bf), full(keys_bf)],
        out_specs=[pl.BlockSpec((tm // 2, d), lambda i: (i, 0)), sel, sel, col, col],
        out_shape=[jax.ShapeDtypeStruct((t // 2, d), jnp.uint32), row_shape, row_shape, col_shape, col_shape],
        scratch_shapes=[pltpu.VMEM((wq_t_bf.shape[0], tm), BF16), pltpu.VMEM((N_KEYS, tm), F32),
                        pltpu.VMEM((N_KEYS, tm), F32)],
        compiler_params=_params("parallel"),
        name="peer_select",
    )(x, g, wq_t_bf, keys_bf)


def _peer_act_kernel(xn_ref, u_ref, o_ref):
    a = _nt(pltpu.bitcast(u_ref[...], BF16), pltpu.bitcast(xn_ref[...], BF16)).astype(BF16)
    o_ref[...] = pltpu.bitcast(0.5 * a * (1.0 + lax.erf(a * INV_SQRT2)), jnp.uint32)


def _peer_act(xn_pk, u_pk, *, tm, eb):
    d = u_pk.shape[1]
    t = 2 * xn_pk.shape[0]
    return pl.pallas_call(
        _peer_act_kernel,
        grid=(t // tm, 2 * u_pk.shape[0] // eb),
        in_specs=[pl.BlockSpec((tm // 2, d), lambda i, e: (i, 0)), pl.BlockSpec((eb // 2, d), lambda i, e: (e, 0))],
        out_specs=pl.BlockSpec((eb // 2, tm), lambda i, e: (e, i)),
        out_shape=jax.ShapeDtypeStruct((u_pk.shape[0], t), jnp.uint32),
        compiler_params=_params("parallel", "parallel"),
        name="peer_act",
    )(xn_pk, u_pk)


def _peer_mix_kernel(act_ref, vt_ref, cnt_ref, e1_ref, rk_ref, e2_ref, xmid_ref, o_ref,
                     wa_scr, acc_scr, *, splits, chunk):
    blk = pl.program_id(1)
    n_heads, groups = cnt_ref.shape[:2]
    eb = vt_ref.shape[1]
    rows_i = eb // N_KEYS

    @pl.when(blk == 0)
    def _():
        acc_scr[...] = jnp.zeros_like(acc_scr)

    def row_tile(ref, h, gi, i_row):
        words = jnp.broadcast_to(ref[h, gi, pl.ds(i_row, 1), :], (chunk // 2, LANES))
        return pltpu.bitcast(words, BF16)

    for s in range(splits):
        part = slice(s * eb // splits, (s + 1) * eb // splits)
        for r in range(s * rows_i // splits, (s + 1) * rows_i // splits):
            i_row = blk * rows_i + r
            for gi in range(groups):
                lanes = slice(gi * LANES, (gi + 1) * LANES)
                for j0 in range(0, N_KEYS, chunk):
                    cols = slice(j0 // 2, (j0 + chunk) // 2)
                    gate = jnp.zeros((chunk, LANES), BF16)
                    for h in range(n_heads):
                        rk = pltpu.bitcast(rk_ref[h, gi, cols, :], BF16)
                        e2 = pltpu.bitcast(e2_ref[h, gi, cols, :], BF16)
                        gate = gate + (jnp.where(rk < row_tile(cnt_ref, h, gi, i_row), e2, jnp.zeros_like(e2))
                                       * row_tile(e1_ref, h, gi, i_row))
                    e0 = r * N_KEYS + j0
                    act = pltpu.bitcast(act_ref[e0 // 2:(e0 + chunk) // 2, lanes], BF16)
                    wa_scr[e0:e0 + chunk, lanes] = gate * act
        acc_scr[...] += _mm(pltpu.bitcast(vt_ref[:, part], BF16), wa_scr[part, :])

    @pl.when(blk == pl.num_programs(1) - 1)
    def _():
        o_ref[...] = xmid_ref[...] + acc_scr[...].T


def _peer_mix(act_pk, vt_pk, cnt, e1, rk, e2, x_mid, *, tm, eb):
    t, d = x_mid.shape
    n_blocks = 2 * act_pk.shape[0] // eb
    n_heads = cnt.shape[0]
    groups = tm // LANES
    tok = pl.BlockSpec((tm, d), lambda i, e: (i, 0))
    sel = pl.BlockSpec((n_heads, groups, N_KEYS, LANES), lambda i, e: (0, i, 0, 0))
    col = pl.BlockSpec((n_heads, groups, N_KEYS // 2, LANES), lambda i, e: (0, i, 0, 0))
    return pl.pallas_call(
        functools.partial(_peer_mix_kernel, splits=2, chunk=N_KEYS),
        grid=(t // tm, n_blocks),
        in_specs=[pl.BlockSpec((eb // 2, tm), lambda i, e: (e, i)),
                  pl.BlockSpec((d // 2, eb), lambda i, e: (0, e)),
                  sel, sel, col, col, tok],
        out_specs=tok,
        out_shape=jax.ShapeDtypeStruct((t, d), F32),
        scratch_shapes=[pltpu.VMEM((eb, tm), BF16), pltpu.VMEM((d, tm), F32)],
        compiler_params=_params("parallel", "arbitrary"),
        name="peer_mix",
    )(act_pk, vt_pk, cnt, e1, rk, e2, x_mid)


def _pack_kernel(x_ref, o_ref):
    o_ref[...] = pltpu.bitcast(x_ref[...].astype(BF16), jnp.uint32)


def _pack_bf16(x, *, tr):
    r, c = x.shape
    return pl.pallas_call(
        _pack_kernel,
        grid=(r // tr,),
        in_specs=[pl.BlockSpec((tr, c), lambda i: (i, 0))],
        out_specs=pl.BlockSpec((tr // 2, c), lambda i: (i, 0)),
        out_shape=jax.ShapeDtypeStruct((r // 2, c), jnp.uint32),
        compiler_params=_params("parallel"),
        name="pack_bf16",
    )(x)


def _block_diag(width):
    idx = jnp.arange(width) // HEAD_DIM
    return jnp.where(idx[:, None] == idx[None, :], 1.0 / HEAD_DIM, 0.0).astype(BF16)


def _tile(n, want):
    return want if n % want == 0 else n


def kernel(x_prompt, x_sample, mem_prompt, cache_sb_k, cache_sb_v, page_table, cache_mem_k, cache_mem_v,
           state_conv, g_mix, w_in, sb_bias, sb_out_g, conv_w, conv_b, conv_ln_g, conv_ln_b, g_mem, w_mem_k,
           w_mem_v, mem_q_g, mem_k_g, mem_out_g, w_out, g_ffn, peer_w_q, peer_sub_keys, peer_u, peer_v):
    depth = w_in.shape[0]
    batch, seq, d = x_prompt.shape
    nb, n_new, _ = x_sample.shape
    sb_heads = sb_bias.shape[1]
    sbw = sb_heads * HEAD_DIM
    cch = conv_w.shape[2]
    taps = conv_w.shape[1]
    mem_heads = mem_out_g.shape[1]
    memw = mem_heads * HEAD_DIM
    mem_tok = mem_prompt.shape[1]
    n_pool, page = cache_sb_k.shape[1:3]
    n_exp = peer_u.shape[1]

    bd_mem = _block_diag(memw)
    bd_sb = _block_diag(sbw)
    xp = x_prompt.reshape(batch * seq, d)
    xs = x_sample.reshape(nb * n_new, d)
    tp = _tile(batch * seq, 512)
    tsm = _tile(nb * n_new, 512)
    outs = [[] for _ in range(8)]

    for l in range(depth):
        row = lambda a: a[l].reshape(1, -1)
        w_in_bf = w_in[l].astype(BF16)
        bias2 = sb_bias[l] * LOG2E
        w_out_bf = w_out[l].astype(BF16)
        gq = jnp.tile(mem_q_g[l], mem_heads).reshape(1, memw)
        gk = jnp.tile(mem_k_g[l], mem_heads).reshape(1, memw)
        conv_args = (conv_w[l], row(conv_b), row(conv_ln_g), row(conv_ln_b))

        mk, mv = _memkv(mem_prompt.reshape(batch * mem_tok, d), row(g_mem), w_mem_k[l].astype(BF16),
                        w_mem_v[l].astype(BF16), bd_mem, gk, tm=_tile(batch * mem_tok, 512))

        q, k, v, kb, vb, glu, qm = _inproj(xp, row(g_mix), w_in_bf, bd_mem, gq, sbw=sbw, cch=cch, memw=memw, tm=tp)
        o_sb = _sb_prompt(q, kb, vb, bias2, row(sb_out_g), batch=batch, seq=seq,
                          bq=_tile(seq, 512), sub=256)
        glu3 = glu.reshape(batch, seq, cch)
        o_conv = _conv_module(glu3, jnp.zeros((batch, CONV_PAD, cch), F32), *conv_args, ts=_tile(seq, 256))
        o_mem = _mem_attn(qm.reshape(batch, seq, memw), mk.reshape(batch, mem_tok, memw),
                          mv.reshape(batch, mem_tok, memw), bd_mem, row(mem_out_g), tq=_tile(seq, 512))
        xp_mid = _outproj(xp, o_sb, o_conv.reshape(-1, cch), o_mem.reshape(-1, memw), w_out_bf, tm=tp)

        q_s, k_s, v_s, kb_s, vb_s, glu_s, qm_s = _inproj(xs, row(g_mix), w_in_bf, bd_mem, gq,
                                                        sbw=sbw, cch=cch, memw=memw, tm=tsm)
        npg = page_table.shape[1]
        o_sb_s = _sb_sample(q_s, kb_s, vb_s, bias2, row(sb_out_g), bd_sb,
                            cache_sb_k, cache_sb_v, page_table, layer=l, n_new=n_new,
                            pages=16 if npg % 16 == 0 else npg)
        glu_s3 = glu_s.reshape(nb, n_new, cch)
        hist_s = jnp.pad(state_conv[l], ((0, 0), (CONV_PAD - (taps - 1), 0), (0, 0)))
        o_conv_s = _conv_module(glu_s3, hist_s, *conv_args, ts=n_new)
        o_mem_s = _mem_attn(qm_s.reshape(nb, n_new, memw), cache_mem_k[l].reshape(nb, mem_tok, memw),
                            cache_mem_v[l].reshape(nb, mem_tok, memw), bd_mem, row(mem_out_g), tq=n_new)
        xs_mid = _outproj(xs, o_sb_s.reshape(-1, sbw), o_conv_s.reshape(-1, cch), o_mem_s.reshape(-1, memw),
                          w_out_bf, tm=tsm)

        wq_t = peer_w_q[l].T.astype(BF16)
        keys = peer_sub_keys[l].reshape(-1, N_KEYS, peer_sub_keys.shape[-1]).astype(BF16)
        u_pk = _pack_bf16(peer_u[l], tr=_tile(n_exp, 1024))
        vt_pk = _pack_bf16(peer_v[l].T, tr=64)
        new = []
        for x_mid, tm in ((xp_mid, tp), (xs_mid, tsm)):
            xn_pk, cnt, e1, rk, e2 = _peer_select(x_mid, row(g_ffn), wq_t, keys, tm=tm)
            act_pk = _peer_act(xn_pk, u_pk, tm=_tile(x_mid.shape[0], 1024), eb=_tile(n_exp, 2048))
            new.append(_peer_mix(act_pk, vt_pk, cnt, e1, rk, e2, x_mid, tm=tm, eb=_tile(n_exp, 2048)))
        xp, xs = new

        hist_rows = taps - 1
        cs = jnp.concatenate([state_conv[l], glu_s3], axis=1)[:, -hist_rows:]
        for lst, val in zip(outs, (k.reshape(batch, seq, sb_heads, HEAD_DIM), v.reshape(batch, seq, sb_heads, HEAD_DIM),
                                   k_s.reshape(nb, n_new, sb_heads, HEAD_DIM), v_s.reshape(nb, n_new, sb_heads, HEAD_DIM),
                                   mk.reshape(batch, mem_tok, mem_heads, HEAD_DIM),
                                   mv.reshape(batch, mem_tok, mem_heads, HEAD_DIM),
                                   glu3[:, seq - hist_rows:], cs)):
            lst.append(val)

    return (xp.reshape(batch, seq, d), xs.reshape(nb, n_new, d), *[jnp.stack(o) for o in outs])
```

```python
import functools
import math

import jax
import jax.numpy as jnp
from jax import lax
from jax.experimental import pallas as pl
from jax.experimental.pallas import tpu as pltpu

F32 = jnp.float32
BF16 = jnp.bfloat16

EPS = 1e-6
HEAD_DIM = 64
LANES = 128
TOPK = 16
N_KEYS = 128
CONV_PAD = 32
VMEM_LIMIT = 56 * 1024 * 1024
NEG_INF = float("-inf")
POS_INF = float("inf")
INV_SQRT2 = 1.0 / math.sqrt(2.0)
LOG2E = 1.0 / math.log(2.0)


def _nt(a, b):
    return lax.dot_general(a, b, (((1,), (1,)), ((), ())), preferred_element_type=F32)


def _mm(a, b):
    return jnp.dot(a, b, preferred_element_type=F32)


def _group_mean_sq(x, bd):
    sq = x * x
    hi = sq.astype(BF16)
    lo = (sq - hi.astype(F32)).astype(BF16)
    return _mm(hi, bd) + _mm(lo, bd)


def _params(*sem):
    return pltpu.CompilerParams(dimension_semantics=sem, vmem_limit_bytes=VMEM_LIMIT)


def _inproj_kernel(x_ref, g_ref, w_ref, bd_ref, gq_ref,
                   q_ref, k_ref, v_ref, kb_ref, vb_ref, glu_ref, qm_ref, *, sbw, cch):
    x = x_ref[...]
    xn = x * lax.rsqrt(jnp.mean(x * x, axis=-1, keepdims=True) + EPS) * g_ref[...]
    z = _mm(xn.astype(BF16), w_ref[...])
    q_ref[...] = (z[:, :sbw] * (LOG2E * HEAD_DIM ** -0.5)).astype(BF16)
    k = z[:, sbw:2 * sbw]
    v = z[:, 2 * sbw:3 * sbw]
    k_ref[...] = k
    v_ref[...] = v
    kb_ref[...] = k.astype(BF16)
    vb_ref[...] = v.astype(BF16)
    o = 3 * sbw
    glu_ref[...] = z[:, o:o + cch] * jax.nn.sigmoid(z[:, o + cch:o + 2 * cch])
    qm = z[:, o + 2 * cch:]
    qm_ref[...] = (qm * lax.rsqrt(_group_mean_sq(qm, bd_ref[...]) + EPS) * gq_ref[...]).astype(BF16)


def _inproj(x, g, w_bf, bd_mem, gq, *, sbw, cch, memw, tm):
    t, d = x.shape
    cols = w_bf.shape[1]
    row = lambda width: pl.BlockSpec((tm, width), lambda i: (i, 0))
    full = lambda a: pl.BlockSpec(a.shape, lambda i: (0,) * a.ndim)
    return pl.pallas_call(
        functools.partial(_inproj_kernel, sbw=sbw, cch=cch),
        grid=(t // tm,),
        in_specs=[row(d), full(g), full(w_bf), full(bd_mem), full(gq)],
        out_specs=[row(sbw), row(sbw), row(sbw), row(sbw), row(sbw), row(cch), row(memw)],
        out_shape=[jax.ShapeDtypeStruct((t, sbw), BF16), jax.ShapeDtypeStruct((t, sbw), F32),
                   jax.ShapeDtypeStruct((t, sbw), F32), jax.ShapeDtypeStruct((t, sbw), BF16),
                   jax.ShapeDtypeStruct((t, sbw), BF16), jax.ShapeDtypeStruct((t, cch), F32),
                   jax.ShapeDtypeStruct((t, memw), BF16)],
        compiler_params=_params("parallel"),
        name="inproj",
    )(x, g, w_bf, bd_mem, gq)


def _sb_weights(z, tri, car, mask):
    ks, sub = z.shape[1], tri.shape[0]
    neg_abs = lax.bitcast_convert_type(lax.bitcast_convert_type(z, jnp.uint32) | jnp.uint32(0x80000000), F32)
    lb = jnp.minimum(z, 0.0) - jnp.log2(1.0 + jnp.exp2(neg_abs))
    l1m = lb - z
    if mask is not None:
        l1m = jnp.where(mask, l1m, 0.0)
    ws = [None] * (ks // sub)
    for j in reversed(range(ks // sub)):
        cols = slice(j * sub, (j + 1) * sub)
        lj = l1m[:, cols].astype(BF16)
        tail = _mm(lj, tri)
        w = jnp.exp2(lb[:, cols] + tail + jnp.tile(car, (1, sub // car.shape[1])))
        if mask is not None:
            w = jnp.where(mask[:, cols], w, 0.0)
        ws[j] = w.astype(BF16)
        total = tail[:, 0:1] + lj[:, 0:1].astype(F32)
        car = car + jnp.broadcast_to(total, car.shape)
    return (ws[0] if len(ws) == 1 else jnp.concatenate(ws, axis=1)), car


def _sb_prompt_kernel(bias_ref, q_ref, k_ref, v_ref, tri_ref, g_ref, o_ref, acc, car_a, car_b, *, bq):
    hp = pl.program_id(1)
    qi = pl.program_id(2)
    q2 = q_ref[...]
    lane = lax.broadcasted_iota(jnp.int32, (bq, LANES), 1)
    first = lane < HEAD_DIM
    zero = jnp.zeros_like(q2)

    def with_bias(qh, bias):
        b = jnp.full((bq, LANES), bias, F32)
        hi = b.astype(BF16).astype(F32)
        extra = jnp.where(lane == 0, hi, jnp.where(lane == 1, b - hi, 0.0))
        return jnp.concatenate([qh, extra.astype(BF16)], axis=1)

    qa = with_bias(jnp.where(first, q2, zero), bias_ref[2 * hp])
    qb = with_bias(jnp.where(first, zero, q2), bias_ref[2 * hp + 1])
    ones = jnp.where(lane < 2, 1.0, 0.0).astype(BF16)
    tri = tri_ref[...]
    for r in (acc, car_a, car_b):
        r[...] = jnp.zeros_like(r)

    def step(kb, mask):
        off = pl.multiple_of(kb * bq, bq)
        kblk = jnp.concatenate([k_ref[pl.ds(off, bq), :], ones], axis=1)
        vblk = v_ref[pl.ds(off, bq), :]
        vzero = jnp.zeros_like(vblk)
        wa, car_a[...] = _sb_weights(_nt(qa, kblk), tri, car_a[...], mask)
        wb, car_b[...] = _sb_weights(_nt(qb, kblk), tri, car_b[...], mask)
        acc[...] += _mm(wa, jnp.where(first, vblk, vzero)) + _mm(wb, jnp.where(first, vzero, vblk))

    rows = lax.broadcasted_iota(jnp.int32, (bq, bq), 0)
    cols = lax.broadcasted_iota(jnp.int32, (bq, bq), 1)
    step(qi, cols < rows)

    def body(i, c):
        step(qi - 1 - i, None)
        return c

    lax.fori_loop(0, qi, body, 0)

    o = acc[...]
    sq = o * o
    ms_a = jnp.sum(jnp.where(first, sq, 0.0), axis=-1, keepdims=True)
    ms_b = jnp.sum(jnp.where(first, 0.0, sq), axis=-1, keepdims=True)
    ms = jnp.where(first, ms_a, ms_b) * (1.0 / HEAD_DIM)
    o_ref[...] = (o * lax.rsqrt(ms + EPS) * g_ref[...]).astype(BF16)


def _tri_matrix(n):
    return (jnp.arange(n)[:, None] > jnp.arange(n)[None, :]).astype(BF16)


def _sb_prompt(q_bf, k_bf, v_bf, bias, g_row, *, batch, seq, bq, sub):
    t, sbw = q_bf.shape
    npair = sbw // LANES
    nq = seq // bq
    return pl.pallas_call(
        functools.partial(_sb_prompt_kernel, bq=bq),
        grid=(batch, npair, nq),
        in_specs=[pl.BlockSpec(memory_space=pltpu.SMEM),
                  pl.BlockSpec((bq, LANES), lambda b, p, i: (b * nq + i, p)),
                  pl.BlockSpec((seq, LANES), lambda b, p, i: (b, p)),
                  pl.BlockSpec((seq, LANES), lambda b, p, i: (b, p)),
                  pl.BlockSpec((sub, sub), lambda b, p, i: (0, 0)),
                  pl.BlockSpec((1, LANES), lambda b, p, i: (0, p))],
        out_specs=pl.BlockSpec((bq, LANES), lambda b, p, i: (b * nq + i, p)),
        out_shape=jax.ShapeDtypeStruct((t, sbw), BF16),
        scratch_shapes=[pltpu.VMEM((bq, LANES), F32)] * 3,
        compiler_params=_params("parallel", "parallel", "arbitrary"),
        name="sb_prompt",
    )(bias, q_bf, k_bf, v_bf, _tri_matrix(sub), g_row)


def _sb_sample_kernel(pt_ref, q_ref, kn_ref, vn_ref, bias_ref, tri_ref, bd_ref, g_ref, *rest,
                      pages, n_heads, n_new):
    k_refs = rest[:pages]
    v_refs = rest[pages:2 * pages]
    o_ref = rest[2 * pages]
    acc, car = rest[2 * pages + 1:]
    s = pl.program_id(1)
    q = q_ref[0]
    bias = bias_ref[...]
    tri = tri_ref[...]
    bk = tri.shape[0]
    nrow, width = q.shape

    @pl.when(s == 0)
    def _():
        pad = jnp.zeros((bk - kn_ref.shape[1], width), BF16)
        kblk = jnp.concatenate([kn_ref[0], pad], axis=0)
        vblk = jnp.concatenate([vn_ref[0], pad], axis=0)
        rows = lax.broadcasted_iota(jnp.int32, (nrow, bk), 0)
        cols = lax.broadcasted_iota(jnp.int32, (nrow, bk), 1)
        mask = cols < rows // n_heads
        w, car[...] = _sb_weights(_nt(q, kblk) + bias[:, :bk], tri, jnp.zeros(car.shape, F32), mask)
        acc[...] = _mm(w, vblk)

    kt = jnp.concatenate([k_refs[r][0].astype(BF16) for r in reversed(range(pages))], axis=1)
    vt = jnp.concatenate([v_refs[r][0].astype(BF16) for r in reversed(range(pages))], axis=1)
    w, car[...] = _sb_weights(_mm(q, kt) + bias, tri, car[...], None)
    acc[...] += _nt(w, vt)

    @pl.when(s == pl.num_programs(1) - 1)
    def _():
        rows = lax.broadcasted_iota(jnp.int32, (nrow, width), 0)
        cols = lax.broadcasted_iota(jnp.int32, (nrow, width), 1)
        own = jnp.where(cols // HEAD_DIM == rows % n_heads, acc[...], 0.0)
        o = jnp.sum(own.reshape(n_new, n_heads, width), axis=1)
        o = o * lax.rsqrt(_group_mean_sq(o, bd_ref[...]) + EPS) * g_ref[...]
        o_ref[0] = o.astype(BF16)


def _sb_sample(q_bf, kn_bf, vn_bf, bias, g_row, bd_sb, pool_k, pool_v, page_table, *, layer, n_new, pages):
    nb, npg = page_table.shape
    tok, n_heads = pool_k.shape[2:4]
    width = n_heads * HEAD_DIM
    nrow = n_new * n_heads
    nstep = npg // pages
    q3 = q_bf.reshape(nb, n_new, width)
    head_of_lane = jnp.arange(width) // HEAD_DIM
    qbd = jnp.where(head_of_lane[None, None, None, :] == jnp.arange(n_heads)[None, None, :, None],
                    q3[:, :, None, :], jnp.zeros((), BF16)).reshape(nb, nrow, width)
    pad_new = 8 - n_new
    kn = jnp.pad(kn_bf.reshape(nb, n_new, width), ((0, 0), (0, pad_new), (0, 0)))
    vn = jnp.pad(vn_bf.reshape(nb, n_new, width), ((0, 0), (0, pad_new), (0, 0)))
    bias_rows = jnp.broadcast_to(jnp.tile(bias, n_new)[:, None], (nrow, pages * tok)).astype(F32)

    def token_minor(pool):
        return jnp.transpose(pool, (0, 1, 3, 4, 2)).reshape(*pool.shape[:2], width, tok)

    def page_spec(r):
        return pl.BlockSpec((None, 1, width, tok),
                            lambda b, s, pt: (layer, pt[b, npg - 1 - (s * pages + r)], 0, 0))

    const = lambda shape: pl.BlockSpec(shape, lambda b, s, pt: (0,) * len(shape))
    per_b = lambda shape: pl.BlockSpec(shape, lambda b, s, pt: (b,) + (0,) * (len(shape) - 1))
    grid_spec = pltpu.PrefetchScalarGridSpec(
        num_scalar_prefetch=1,
        grid=(nb, nstep),
        in_specs=[per_b((1, nrow, width)), per_b((1, 8, width)), per_b((1, 8, width)),
                  const((nrow, pages * tok)), const((tok, tok)), const((width, width)), const((1, width))]
                 + [page_spec(r) for r in range(pages)] * 2,
        out_specs=per_b((1, n_new, width)),
        scratch_shapes=[pltpu.VMEM((nrow, width), F32), pltpu.VMEM((nrow, tok), F32)],
    )
    return pl.pallas_call(
        functools.partial(_sb_sample_kernel, pages=pages, n_heads=n_heads, n_new=n_new),
        grid_spec=grid_spec,
        out_shape=jax.ShapeDtypeStruct((nb, n_new, width), BF16),
        compiler_params=_params("parallel", "arbitrary"),
        name="sb_sample",
    )(page_table, qbd, kn, vn, bias_rows, _tri_matrix(tok), bd_sb, g_row,
      *([token_minor(pool_k)] * pages), *([token_minor(pool_v)] * pages))


def _conv_kernel(*refs, ts, taps, chunk, has_prev):
    if has_prev:
        hist_ref, prev_ref, cur_ref, w_ref, b_ref, lg_ref, lb_ref, o_ref, buf = refs
        head = jnp.where(pl.program_id(1) == 0, hist_ref[0], prev_ref[0])
    else:
        hist_ref, cur_ref, w_ref, b_ref, lg_ref, lb_ref, o_ref, buf = refs
        head = hist_ref[0]
    buf[0:CONV_PAD, :] = head
    buf[CONV_PAD:CONV_PAD + ts, :] = cur_ref[0]
    first = CONV_PAD - (taps - 1)
    for c0 in range(0, ts, chunk):
        acc = jnp.zeros((chunk, buf.shape[1]), F32)
        for k in range(taps):
            acc = acc + w_ref[k:k + 1, :] * buf[first + c0 + k:first + c0 + k + chunk, :]
        c = acc + b_ref[...]
        xc = c - jnp.mean(c, axis=-1, keepdims=True)
        y = xc * lax.rsqrt(jnp.mean(xc * xc, axis=-1, keepdims=True) + EPS) * lg_ref[...] + lb_ref[...]
        o_ref[0, c0:c0 + chunk, :] = (y * jax.nn.sigmoid(y)).astype(BF16)


def _conv_module(glu3, hist_pad, w, b, lg, lb, *, ts):
    nb, s, cch = glu3.shape
    taps = w.shape[0]
    has_prev = s > ts
    chunk = min(ts, 64)
    cur = pl.BlockSpec((1, ts, cch), lambda bb, i: (bb, i, 0))
    hist = pl.BlockSpec((1, CONV_PAD, cch), lambda bb, i: (bb, 0, 0))
    const = lambda a: pl.BlockSpec(a.shape, lambda bb, i: (0,) * a.ndim)
    in_specs, args = [hist], [hist_pad]
    if has_prev:
        per = ts // CONV_PAD
        in_specs.append(pl.BlockSpec((1, CONV_PAD, cch), lambda bb, i: (bb, jnp.maximum(i * per - 1, 0), 0)))
        args.append(glu3)
    in_specs += [cur, const(w), const(b), const(lg), const(lb)]
    args += [glu3, w, b, lg, lb]
    return pl.pallas_call(
        functools.partial(_conv_kernel, ts=ts, taps=taps, chunk=chunk, has_prev=has_prev),
        grid=(nb, s // ts),
        in_specs=in_specs,
        out_specs=cur,
        out_shape=jax.ShapeDtypeStruct((nb, s, cch), BF16),
        scratch_shapes=[pltpu.VMEM((CONV_PAD + max(ts, 8), cch), F32)],
        compiler_params=_params("parallel", "parallel"),
        name="conv_module",
    )(*args)


def _memkv_kernel(m_ref, g_ref, wk_ref, wv_ref, bd_ref, gk_ref, k_ref, v_ref):
    x = m_ref[...]
    xn = (x * lax.rsqrt(jnp.mean(x * x, axis=-1, keepdims=True) + EPS) * g_ref[...]).astype(BF16)
    k = _mm(xn, wk_ref[...])
    k_ref[...] = k * lax.rsqrt(_group_mean_sq(k, bd_ref[...]) + EPS) * gk_ref[...]
    v_ref[...] = _mm(xn, wv_ref[...])


def _memkv(mem, g, wk_bf, wv_bf, bd_mem, gk, *, tm):
    t, d = mem.shape
    memw = wk_bf.shape[1]
    full = lambda a: pl.BlockSpec(a.shape, lambda i: (0,) * a.ndim)
    out = pl.BlockSpec((tm, memw), lambda i: (i, 0))
    return pl.pallas_call(
        _memkv_kernel,
        grid=(t // tm,),
        in_specs=[pl.BlockSpec((tm, d), lambda i: (i, 0)), full(g), full(wk_bf), full(wv_bf), full(bd_mem), full(gk)],
        out_specs=[out, out],
        out_shape=[jax.ShapeDtypeStruct((t, memw), F32)] * 2,
        compiler_params=_params("parallel"),
        name="memkv",
    )(mem, g, wk_bf, wv_bf, bd_mem, gk)


def _mem_attn_kernel(q_ref, k_ref, v_ref, bd_ref, g_ref, o_ref, *, n_heads):
    q = q_ref[0]
    tq = q.shape[0]
    rows = max(tq, 8)
    if rows != tq:
        q = jnp.concatenate([q, jnp.zeros((rows - tq, q.shape[1]), q.dtype)], axis=0)
    kb = k_ref[0].astype(BF16)
    vb = v_ref[0].astype(BF16)
    head = lax.broadcasted_iota(jnp.int32, q.shape, 1) // HEAD_DIM
    zero = jnp.zeros_like(q)
    out = jnp.zeros(q.shape, F32)
    for h in range(n_heads):
        s = _nt(jnp.where(head == h, q, zero), kb) * (HEAD_DIM ** -0.5)
        p = jnp.exp(s - jnp.max(s, axis=-1, keepdims=True))
        oh = _mm(p.astype(BF16), vb) / jnp.sum(p, axis=-1, keepdims=True)
        out = jnp.where(head == h, oh, out)
    out = out * lax.rsqrt(_group_mean_sq(out, bd_ref[...]) + EPS) * g_ref[...]
    o_ref[0] = out[:tq].astype(BF16)


def _mem_attn(q3, k3, v3, bd_mem, g_row, *, tq):
    ng, s, memw = q3.shape
    m = k3.shape[1]
    const = lambda a: pl.BlockSpec(a.shape, lambda gq, i: (0,) * a.ndim)
    qspec = pl.BlockSpec((1, tq, memw), lambda gq, i: (gq, i, 0))
    kspec = pl.BlockSpec((1, m, memw), lambda gq, i: (gq, 0, 0))
    return pl.pallas_call(
        functools.partial(_mem_attn_kernel, n_heads=memw // HEAD_DIM),
        grid=(ng, s // tq),
        in_specs=[qspec, kspec, kspec, const(bd_mem), const(g_row)],
        out_specs=qspec,
        out_shape=jax.ShapeDtypeStruct((ng, s, memw), BF16),
        compiler_params=_params("parallel", "parallel"),
        name="mem_attn",
    )(q3, k3, v3, bd_mem, g_row)


def _outproj_kernel(x_ref, sb_ref, cv_ref, mm_ref, w_ref, o_ref):
    a = sb_ref.shape[1]
    b = a + cv_ref.shape[1]
    y = _mm(sb_ref[...], w_ref[0:a, :]) + _mm(cv_ref[...], w_ref[a:b, :]) + _mm(mm_ref[...], w_ref[b:, :])
    o_ref[...] = x_ref[...] + y


def _outproj(x, o_sb, o_conv, o_mem, w_bf, *, tm):
    t, d = x.shape
    row = lambda a: pl.BlockSpec((tm, a.shape[1]), lambda i: (i, 0))
    return pl.pallas_call(
        _outproj_kernel,
        grid=(t // tm,),
        in_specs=[row(x), row(o_sb), row(o_conv), row(o_mem), pl.BlockSpec(w_bf.shape, lambda i: (0, 0))],
        out_specs=row(x),
        out_shape=jax.ShapeDtypeStruct((t, d), F32),
        compiler_params=_params("parallel"),
        name="outproj",
    )(x, o_sb, o_conv, o_mem, w_bf)


def _top_values(s, n, with_rank=False):
    tops = []
    cur = s
    rank = jnp.full(s.shape, float(n), F32)
    for k in range(n):
        m = jnp.max(cur, axis=0, keepdims=True)
        tops.append(m)
        hit = cur >= jnp.where(m == NEG_INF, POS_INF, m)
        if with_rank:
            rank = jnp.where(hit, float(k), rank)
        cur = jnp.where(hit, NEG_INF, cur)
    return (tops, rank) if with_rank else tops


def _peer_select_tile(s1, s2):
    t1 = jnp.concatenate(_top_values(s1, TOPK), axis=0)
    tops2, rank2 = _top_values(s2, TOPK, with_rank=True)
    t2 = jnp.concatenate(tops2, axis=0)
    row8 = lax.broadcasted_iota(jnp.int32, (8, LANES), 0)
    parts = [t1[0:1] + t2]
    for a in range(1, 8):
        parts.append(jnp.where(row8 < TOPK // (a + 1), t1[a:a + 1] + t2[0:8], NEG_INF))
    parts.append(t1[8:TOPK] + t2[0:1])
    best = _top_values(jnp.concatenate(parts, axis=0), TOPK)
    tau = best[TOPK - 1]
    top = best[0]
    zsum = jnp.zeros_like(top)
    for bv in best:
        zsum = zsum + jnp.exp(bv - top)
    cnt = jnp.zeros(s1.shape, F32)
    for a in range(TOPK):
        sel = (t1[a:a + 1] + t2) >= tau
        cnt_a = jnp.sum(jnp.where(sel, 1.0, 0.0), axis=0, keepdims=True)
        cnt = jnp.where(s1 == t1[a:a + 1], cnt_a, cnt)
    e1 = jnp.exp(s1 - t1[0:1]) / zsum
    e2 = jnp.exp(s2 - t2[0:1])
    return cnt, e1, rank2, e2


def _twin_bf16(x):
    hi = lax.bitcast_convert_type(x.astype(BF16).astype(F32), jnp.uint32)
    return hi | (hi >> 16)


def _peer_select_kernel(x_ref, g_ref, wq_ref, keys_ref, xn_ref, cnt_ref, e1_ref, rk_ref, e2_ref,
                        qt_scr, s1_scr, s2_scr, *, n_heads):
    x = x_ref[...]
    xn = (x * lax.rsqrt(jnp.mean(x * x, axis=-1, keepdims=True) + EPS) * g_ref[...]).astype(BF16)
    xn_ref[...] = pltpu.bitcast(xn, jnp.uint32)
    qt_scr[...] = _nt(wq_ref[...], xn).astype(BF16)
    dh = keys_ref.shape[2]
    groups = x.shape[0] // LANES

    def head(h, c):
        r1 = pl.multiple_of(h * 2 * dh, 2 * dh)
        r2 = pl.multiple_of(h * 2 * dh + dh, dh)
        s1_scr[...] = _mm(keys_ref[2 * h], qt_scr[pl.ds(r1, dh), :])
        s2_scr[...] = _mm(keys_ref[2 * h + 1], qt_scr[pl.ds(r2, dh), :])
        for gi in range(groups):
            lanes = slice(gi * LANES, (gi + 1) * LANES)
            cnt, e1, rank2, e2 = _peer_select_tile(s1_scr[:, lanes], s2_scr[:, lanes])
            cnt_ref[h, gi] = _twin_bf16(cnt)
            e1_ref[h, gi] = _twin_bf16(e1)
            rk_ref[h, gi] = pltpu.bitcast(rank2.astype(BF16), jnp.uint32)
            e2_ref[h, gi] = pltpu.bitcast(e2.astype(BF16), jnp.uint32)
        return c

    lax.fori_loop(0, n_heads, head, 0)


def _peer_select(x, g, wq_t_bf, keys_bf, *, tm):
    t, d = x.shape
    n_heads = keys_bf.shape[0] // 2
    groups = tm // LANES
    full = lambda a: pl.BlockSpec(a.shape, lambda i: (0,) * a.ndim)
    sel = pl.BlockSpec((n_heads, groups, N_KEYS, LANES), lambda i: (0, i, 0, 0))
    row_shape = jax.ShapeDtypeStruct((n_heads, t // LANES, N_KEYS, LANES), jnp.uint32)
    col = pl.BlockSpec((n_heads, groups, N_KEYS // 2, LANES), lambda i: (0, i, 0, 0))
    col_shape = jax.ShapeDtypeStruct((n_heads, t // LANES, N_KEYS // 2, LANES), jnp.uint32)
    return pl.pallas_call(
        functools.partial(_peer_select_kernel, n_heads=n_heads),
        grid=(t // tm,),
        in_specs=[pl.BlockSpec((tm, d), lambda i: (i, 0)), full(g), full(wq_t_bf), full(keys_bf)],
        out_specs=[pl.BlockSpec((tm // 2, d), lambda i: (i, 0)), sel, sel, col, col],
        out_shape=[jax.ShapeDtypeStruct((t // 2, d), jnp.uint32), row_shape, row_shape, col_shape, col_shape],
        scratch_shapes=[pltpu.VMEM((wq_t_bf.shape[0], tm), BF16), pltpu.VMEM((N_KEYS, tm), F32),
                        pltpu.VMEM((N_KEYS, tm), F32)],
        compiler_params=_params("parallel"),
        name="peer_select",
    )(x, g, wq_t_bf, keys_bf)


def _peer_act_kernel(xn_ref, u_ref, o_ref):
    a = _nt(pltpu.bitcast(u_ref[...], BF16), pltpu.bitcast(xn_ref[...], BF16)).astype(BF16)
    act = pltpu.bitcast(0.5 * a * (1.0 + lax.erf(a * INV_SQRT2)), jnp.uint32)
    tw = o_ref.shape[2]
    for k in range(o_ref.shape[0]):
        o_ref[k] = act[:, k * tw:(k + 1) * tw]


def _peer_act(xn_pk, u_pk, *, tm, eb, tw):
    d = u_pk.shape[1]
    t = 2 * xn_pk.shape[0]
    return pl.pallas_call(
        _peer_act_kernel,
        grid=(t // tm, 2 * u_pk.shape[0] // eb),
        in_specs=[pl.BlockSpec((tm // 2, d), lambda i, e: (i, 0)), pl.BlockSpec((eb // 2, d), lambda i, e: (e, 0))],
        out_specs=pl.BlockSpec((tm // tw, eb // 2, tw), lambda i, e: (i, e, 0)),
        out_shape=jax.ShapeDtypeStruct((t // tw, u_pk.shape[0], tw), jnp.uint32),
        compiler_params=_params("parallel", "parallel"),
        name="peer_act",
    )(xn_pk, u_pk)


def _peer_mix_kernel(act_ref, vt_ref, cnt_ref, e1_ref, rk_ref, e2_ref, xmid_ref, o_ref,
                     wa_scr, acc_scr, *, splits, chunk):
    blk = pl.program_id(1)
    n_heads, groups = cnt_ref.shape[:2]
    eb = vt_ref.shape[1]
    rows_i = eb // N_KEYS

    @pl.when(blk == 0)
    def _():
        acc_scr[...] = jnp.zeros_like(acc_scr)

    def row_tile(ref, h, gi, i_row):
        words = jnp.broadcast_to(ref[h, gi, pl.ds(i_row, 1), :], (chunk // 2, LANES))
        return pltpu.bitcast(words, BF16)

    for s in range(splits):
        part = slice(s * eb // splits, (s + 1) * eb // splits)
        for r in range(s * rows_i // splits, (s + 1) * rows_i // splits):
            i_row = blk * rows_i + r
            for gi in range(groups):
                lanes = slice(gi * LANES, (gi + 1) * LANES)
                for j0 in range(0, N_KEYS, chunk):
                    cols = slice(j0 // 2, (j0 + chunk) // 2)
                    gate = jnp.zeros((chunk, LANES), BF16)
                    for h in range(n_heads):
                        rk = pltpu.bitcast(rk_ref[h, gi, cols, :], BF16)
                        e2 = pltpu.bitcast(e2_ref[h, gi, cols, :], BF16)
                        gate = gate + (jnp.where(rk < row_tile(cnt_ref, h, gi, i_row), e2, jnp.zeros_like(e2))
                                       * row_tile(e1_ref, h, gi, i_row))
                    e0 = r * N_KEYS + j0
                    act = pltpu.bitcast(act_ref[e0 // 2:(e0 + chunk) // 2, lanes], BF16)
                    wa_scr[e0:e0 + chunk, lanes] = gate * act
        acc_scr[...] += _mm(pltpu.bitcast(vt_ref[:, part], BF16), wa_scr[part, :])

    @pl.when(blk == pl.num_programs(1) - 1)
    def _():
        o_ref[...] = xmid_ref[...] + acc_scr[...].T


def _peer_mix(act_pk, vt_pk, cnt, e1, rk, e2, x_mid, *, tm, eb):
    t, d = x_mid.shape
    n_blocks = vt_pk.shape[0]
    n_heads = cnt.shape[0]
    groups = tm // LANES
    tok = pl.BlockSpec((tm, d), lambda i, e: (i, 0))
    sel = pl.BlockSpec((n_heads, groups, N_KEYS, LANES), lambda i, e: (0, i, 0, 0))
    col = pl.BlockSpec((n_heads, groups, N_KEYS // 2, LANES), lambda i, e: (0, i, 0, 0))
    return pl.pallas_call(
        functools.partial(_peer_mix_kernel, splits=2, chunk=N_KEYS),
        grid=(t // tm, n_blocks),
        in_specs=[pl.BlockSpec((None, eb // 2, tm), lambda i, e: (i, e, 0)),
                  pl.BlockSpec((None, d // 2, eb), lambda i, e: (e, 0, 0)),
                  sel, sel, col, col, tok],
        out_specs=tok,
        out_shape=jax.ShapeDtypeStruct((t, d), F32),
        scratch_shapes=[pltpu.VMEM((eb, tm), BF16), pltpu.VMEM((d, tm), F32)],
        compiler_params=_params("parallel", "arbitrary"),
        name="peer_mix",
    )(act_pk, vt_pk, cnt, e1, rk, e2, x_mid)


def _pack_kernel(x_ref, o_ref):
    o_ref[...] = pltpu.bitcast(x_ref[...].astype(BF16), jnp.uint32)


def _pack_bf16(x, *, tr):
    r, c = x.shape
    return pl.pallas_call(
        _pack_kernel,
        grid=(r // tr,),
        in_specs=[pl.BlockSpec((tr, c), lambda i: (i, 0))],
        out_specs=pl.BlockSpec((tr // 2, c), lambda i: (i, 0)),
        out_shape=jax.ShapeDtypeStruct((r // 2, c), jnp.uint32),
        compiler_params=_params("parallel"),
        name="pack_bf16",
    )(x)


def _block_diag(width):
    idx = jnp.arange(width) // HEAD_DIM
    return jnp.where(idx[:, None] == idx[None, :], 1.0 / HEAD_DIM, 0.0).astype(BF16)


def _tile(n, want):
    return want if n % want == 0 else n


def kernel(x_prompt, x_sample, mem_prompt, cache_sb_k, cache_sb_v, page_table, cache_mem_k, cache_mem_v,
           state_conv, g_mix, w_in, sb_bias, sb_out_g, conv_w, conv_b, conv_ln_g, conv_ln_b, g_mem, w_mem_k,
           w_mem_v, mem_q_g, mem_k_g, mem_out_g, w_out, g_ffn, peer_w_q, peer_sub_keys, peer_u, peer_v):
    depth = w_in.shape[0]
    batch, seq, d = x_prompt.shape
    nb, n_new, _ = x_sample.shape
    sb_heads = sb_bias.shape[1]
    sbw = sb_heads * HEAD_DIM
    cch = conv_w.shape[2]
    taps = conv_w.shape[1]
    mem_heads = mem_out_g.shape[1]
    memw = mem_heads * HEAD_DIM
    mem_tok = mem_prompt.shape[1]
    n_pool, page = cache_sb_k.shape[1:3]
    n_exp = peer_u.shape[1]

    bd_mem = _block_diag(memw)
    bd_sb = _block_diag(sbw)
    xp = x_prompt.reshape(batch * seq, d)
    xs = x_sample.reshape(nb * n_new, d)
    tp = _tile(batch * seq, 512)
    tsm = _tile(nb * n_new, 512)
    outs = [[] for _ in range(8)]

    for l in range(depth):
        row = lambda a: a[l].reshape(1, -1)
        w_in_bf = w_in[l].astype(BF16)
        bias2 = sb_bias[l] * LOG2E
        w_out_bf = w_out[l].astype(BF16)
        gq = jnp.tile(mem_q_g[l], mem_heads).reshape(1, memw)
        gk = jnp.tile(mem_k_g[l], mem_heads).reshape(1, memw)
        conv_args = (conv_w[l], row(conv_b), row(conv_ln_g), row(conv_ln_b))

        mk, mv = _memkv(mem_prompt.reshape(batch * mem_tok, d), row(g_mem), w_mem_k[l].astype(BF16),
                        w_mem_v[l].astype(BF16), bd_mem, gk, tm=_tile(batch * mem_tok, 512))

        q, k, v, kb, vb, glu, qm = _inproj(xp, row(g_mix), w_in_bf, bd_mem, gq, sbw=sbw, cch=cch, memw=memw, tm=tp)
        o_sb = _sb_prompt(q, kb, vb, bias2, row(sb_out_g), batch=batch, seq=seq,
                          bq=_tile(seq, 512), sub=256)
        glu3 = glu.reshape(batch, seq, cch)
        o_conv = _conv_module(glu3, jnp.zeros((batch, CONV_PAD, cch), F32), *conv_args, ts=_tile(seq, 256))
        o_mem = _mem_attn(qm.reshape(batch, seq, memw), mk.reshape(batch, mem_tok, memw),
                          mv.reshape(batch, mem_tok, memw), bd_mem, row(mem_out_g), tq=_tile(seq, 512))
        xp_mid = _outproj(xp, o_sb, o_conv.reshape(-1, cch), o_mem.reshape(-1, memw), w_out_bf, tm=tp)

        q_s, k_s, v_s, kb_s, vb_s, glu_s, qm_s = _inproj(xs, row(g_mix), w_in_bf, bd_mem, gq,
                                                        sbw=sbw, cch=cch, memw=memw, tm=tsm)
        npg = page_table.shape[1]
        o_sb_s = _sb_sample(q_s, kb_s, vb_s, bias2, row(sb_out_g), bd_sb,
                            cache_sb_k, cache_sb_v, page_table, layer=l, n_new=n_new,
                            pages=16 if npg % 16 == 0 else npg)
        glu_s3 = glu_s.reshape(nb, n_new, cch)
        hist_s = jnp.pad(state_conv[l], ((0, 0), (CONV_PAD - (taps - 1), 0), (0, 0)))
        o_conv_s = _conv_module(glu_s3, hist_s, *conv_args, ts=n_new)
        o_mem_s = _mem_attn(qm_s.reshape(nb, n_new, memw), cache_mem_k[l].reshape(nb, mem_tok, memw),
                            cache_mem_v[l].reshape(nb, mem_tok, memw), bd_mem, row(mem_out_g), tq=n_new)
        xs_mid = _outproj(xs, o_sb_s.reshape(-1, sbw), o_conv_s.reshape(-1, cch), o_mem_s.reshape(-1, memw),
                          w_out_bf, tm=tsm)

        wq_t = peer_w_q[l].T.astype(BF16)
        keys = peer_sub_keys[l].reshape(-1, N_KEYS, peer_sub_keys.shape[-1]).astype(BF16)
        u_pk = _pack_bf16(peer_u[l], tr=_tile(n_exp, 1024))
        eb_mix = _tile(n_exp, 4096)
        vt_pk = _pack_bf16(peer_v[l].T, tr=64)
        vt_pk = vt_pk.reshape(d // 2, n_exp // eb_mix, eb_mix).transpose(1, 0, 2)
        new = []
        for x_mid, tm in ((xp_mid, tp), (xs_mid, tsm)):
            xn_pk, cnt, e1, rk, e2 = _peer_select(x_mid, row(g_ffn), wq_t, keys, tm=tm)
            act_pk = _peer_act(xn_pk, u_pk, tm=_tile(x_mid.shape[0], 1024), eb=_tile(n_exp, 2048), tw=tm)
            new.append(_peer_mix(act_pk, vt_pk, cnt, e1, rk, e2, x_mid, tm=tm, eb=eb_mix))
        xp, xs = new

        hist_rows = taps - 1
        cs = jnp.concatenate([state_conv[l], glu_s3], axis=1)[:, -hist_rows:]
        for lst, val in zip(outs, (k.reshape(batch, seq, sb_heads, HEAD_DIM), v.reshape(batch, seq, sb_heads, HEAD_DIM),
                                   k_s.reshape(nb, n_new, sb_heads, HEAD_DIM), v_s.reshape(nb, n_new, sb_heads, HEAD_DIM),
                                   mk.reshape(batch, mem_tok, mem_heads, HEAD_DIM),
                                   mv.reshape(batch, mem_tok, mem_heads, HEAD_DIM),
                                   glu3[:, seq - hist_rows:], cs)):
            lst.append(val)

    return (xp.reshape(batch, seq, d), xs.reshape(nb, n_new, d), *[jnp.stack(o) for o in outs])
```
